```python
import math
import jax
import jax.numpy as jnp
from jax import lax
import numpy as np

D_MODEL = 2048
BATCH = 2
SEQ = 4096
DEPTH = 2
DEC_BATCH = 128
DEC_SEQ = 8
PAST_LEN = 16384
PAGE_SIZE = 128

MLA_HEADS = 4
MLA_Q_LORA = 384
MLA_KV_LORA = 128
MLA_NOPE = 128
MLA_ROPE = 64
MLA_V = 128
ROPE_THETA = 10000.0
FOX_HEADS = 4
FOX_KV_HEADS = 1
FOX_HD = 128
LRU_WIDTH = 512
LRU_BLOCKS = 4
LRU_C = 8.0
GDN_HEADS = 4
GDN_DK = 128
GDN_DV = 128
GDN_CHUNK = 64

CONV_W = 4
Q_BLOCK = 128
NORM_EPS = 1e-6

MLA_WIDTH = MLA_HEADS * MLA_V
FOX_WIDTH = FOX_HEADS * FOX_HD
GDN_WIDTH = GDN_HEADS * GDN_DV
D_MIX = MLA_WIDTH + FOX_WIDTH + LRU_WIDTH + GDN_WIDTH
GDN_CONV_DIM = 2 * GDN_HEADS * GDN_DK + GDN_HEADS * GDN_DV

IN_MLA = MLA_Q_LORA + MLA_KV_LORA + MLA_ROPE
IN_FOX = FOX_WIDTH + 2 * FOX_KV_HEADS * FOX_HD + FOX_HEADS
IN_LRU = 2 * LRU_WIDTH
IN_GDN = GDN_CONV_DIM + GDN_WIDTH + 2 * GDN_HEADS
D_IN = IN_MLA + IN_FOX + IN_LRU + IN_GDN

D_FF = -(-8 * D_MODEL // (3 * 256)) * 256

kernel_name = 'hymba_mla_fox_rglru_gdn_step'


def rms_norm(x, g):
    xf = x.astype(jnp.float32)
    y = xf * lax.rsqrt(jnp.mean(xf * xf, axis=-1, keepdims=True) + NORM_EPS)
    return (y * g.astype(jnp.float32)).astype(x.dtype)


def l2_normalize(x):
    xf = x.astype(jnp.float32)
    return xf * lax.rsqrt(jnp.sum(xf * xf, axis=-1, keepdims=True) + NORM_EPS)


def apply_rope(x, pos):
    half = x.shape[-1] // 2
    freqs = ROPE_THETA ** (-jnp.arange(half, dtype=jnp.float32) / half)
    ang = pos.astype(jnp.float32)[:, None] * freqs[None, :]
    ang = ang.reshape((ang.shape[0],) + (1,) * (x.ndim - 3) + (half,))
    cos, sin = jnp.cos(ang), jnp.sin(ang)
    xf = x.astype(jnp.float32)
    x1, x2 = xf[..., :half], xf[..., half:]
    return jnp.concatenate([x1 * cos - x2 * sin, x1 * sin + x2 * cos], axis=-1).astype(x.dtype)


def causal_conv(x, buf, w, b=None):
    t = x.shape[1]
    xx = jnp.concatenate([buf.astype(x.dtype), x], axis=1)
    y = xx[:, :t] * w[0]
    for j in range(1, CONV_W):
        y = y + xx[:, j:j + t] * w[j]
    if b is not None:
        y = y + b
    return y, xx[:, t:]


def sweep_query_blocks(fn, *q_arrays):
    t = q_arrays[0].shape[1]
    if t <= Q_BLOCK or t % Q_BLOCK:
        return fn(*q_arrays)
    nb = t // Q_BLOCK
    blocks = tuple(jnp.moveaxis(a.reshape((a.shape[0], nb, Q_BLOCK) + a.shape[2:]), 1, 0) for a in q_arrays)
    out = jnp.moveaxis(lax.map(lambda blk: fn(*blk), blocks), 0, 1)
    return out.reshape((out.shape[0], t) + out.shape[3:])


def softmax_two_part(s_past, s_new, new_pos, q_pos):
    s_new = jnp.where(new_pos <= q_pos[..., None], s_new, -jnp.inf)
    p = jax.nn.softmax(jnp.concatenate([s_past, s_new], axis=-1), axis=-1)
    n_past = s_past.shape[-1]
    return p[..., :n_past], p[..., n_past:]


def mla_mixer(u, past_lat, past_rope, q_a_g, w_uq, q_norm_g, kv_norm_g, kpe_norm_g, w_uk, w_uv):
    b, t, _ = u.shape
    n_past = past_lat.shape[1]
    pos = n_past + jnp.arange(t, dtype=jnp.int32)
    q_a, kv_a, k_pe = jnp.split(u, [MLA_Q_LORA, MLA_Q_LORA + MLA_KV_LORA], axis=-1)
    q = rms_norm(jnp.einsum('btr,rhd->bthd', rms_norm(q_a, q_a_g), w_uq), q_norm_g)
    q_pe = apply_rope(q[..., MLA_NOPE:], pos)
    q_lat = jnp.einsum('bthd,chd->bthc', q[..., :MLA_NOPE], w_uk)
    lat = rms_norm(kv_a, kv_norm_g)
    rot = apply_rope(rms_norm(k_pe, kpe_norm_g), pos)
    scale = (MLA_NOPE + MLA_ROPE) ** -0.5

    def attend(ql, qr, qp):
        s_past = (jnp.einsum('bqhc,bsc->bhqs', ql, past_lat) + jnp.einsum('bqhr,bsr->bhqs', qr, past_rope)).astype(jnp.float32) * scale
        s_new = (jnp.einsum('bqhc,bsc->bhqs', ql, lat) + jnp.einsum('bqhr,bsr->bhqs', qr, rot)).astype(jnp.float32) * scale
        p_past, p_new = softmax_two_part(s_past, s_new, pos, qp[:, None, :])
        return (jnp.einsum('bhqs,bsc->bqhc', p_past.astype(lat.dtype), past_lat)
                + jnp.einsum('bhqs,bsc->bqhc', p_new.astype(lat.dtype), lat))

    o_lat = sweep_query_blocks(attend, q_lat, q_pe, pos[None, :])
    o = jnp.einsum('bqhc,chd->bqhd', o_lat, w_uv)
    return o.reshape(b, t, MLA_WIDTH), lat, rot


def fox_mixer(u, past_k, past_v, past_logf, q_norm_g, k_norm_g, f_bias):
    b, t, _ = u.shape
    n_past = past_k.shape[1]
    grp = FOX_HEADS // FOX_KV_HEADS
    pos = n_past + jnp.arange(t, dtype=jnp.int32)
    kv_w = FOX_KV_HEADS * FOX_HD
    q, k, v, f_logit = jnp.split(u, [FOX_WIDTH, FOX_WIDTH + kv_w, FOX_WIDTH + 2 * kv_w], axis=-1)
    q = rms_norm(q.reshape(b, t, FOX_KV_HEADS, grp, FOX_HD), q_norm_g)
    k = rms_norm(k.reshape(b, t, FOX_KV_HEADS, FOX_HD), k_norm_g)
    v = v.reshape(b, t, FOX_KV_HEADS, FOX_HD)
    logf = jax.nn.log_sigmoid((f_logit + f_bias).astype(jnp.float32))
    cum = jnp.cumsum(jnp.concatenate([past_logf.astype(jnp.float32), logf], axis=1), axis=1)
    cum_k = jnp.moveaxis(cum.reshape(b, n_past + t, FOX_KV_HEADS, grp), 1, -1)
    cum_past, cum_new = cum_k[..., :n_past], cum_k[..., n_past:]
    cum_q = cum[:, n_past:].reshape(b, t, FOX_KV_HEADS, grp)
    scale = FOX_HD ** -0.5

    def attend(qb, cq, qp):
        cq = jnp.moveaxis(cq, 1, -1)[..., None]
        s_past = jnp.einsum('bqkgd,bskd->bkgqs', qb, past_k).astype(jnp.float32) * scale + (cq - cum_past[..., None, :])
        s_new = jnp.einsum('bqkgd,bskd->bkgqs', qb, k).astype(jnp.float32) * scale + (cq - cum_new[..., None, :])
        p_past, p_new = softmax_two_part(s_past, s_new, pos, qp[:, None, None, :])
        return (jnp.einsum('bkgqs,bskd->bqkgd', p_past.astype(v.dtype), past_v)
                + jnp.einsum('bkgqs,bskd->bqkgd', p_new.astype(v.dtype), v))

    o = sweep_query_blocks(attend, q, cum_q, pos[None, :])
    return o.reshape(b, t, FOX_WIDTH), k, v, logf.astype(u.dtype)


def lru_mixer(u, h0, conv0, conv_w, conv_b, w_a, b_a, w_x, b_x, lam):
    b, t, _ = u.shape
    xb, gate = jnp.split(u, 2, axis=-1)
    xc, conv_new = causal_conv(xb, conv0, conv_w, conv_b)
    xblk = xc.reshape(b, t, LRU_BLOCKS, LRU_WIDTH // LRU_BLOCKS)
    r = jax.nn.sigmoid((jnp.einsum('btni,nij->btnj', xblk, w_a).reshape(b, t, LRU_WIDTH) + b_a).astype(jnp.float32))
    i = jax.nn.sigmoid((jnp.einsum('btni,nij->btnj', xblk, w_x).reshape(b, t, LRU_WIDTH) + b_x).astype(jnp.float32))
    log_a = -LRU_C * r * jax.nn.softplus(-lam.astype(jnp.float32))
    a = jnp.exp(log_a)
    inp = jnp.sqrt(-jnp.expm1(2.0 * log_a)) * (i * xc.astype(jnp.float32))

    def step(h, ab):
        h = ab[0] * h + ab[1]
        return h, h

    h_last, hs = lax.scan(step, h0.astype(jnp.float32), (jnp.swapaxes(a, 0, 1), jnp.swapaxes(inp, 0, 1)))
    y = jnp.swapaxes(hs, 0, 1).astype(u.dtype) * jax.nn.gelu(gate)
    return y, h_last.astype(h0.dtype), conv_new


def chunked_gated_delta(q, k, v, g, beta, s0):
    b, t, h, dk = q.shape
    dv = v.shape[-1]
    c = math.gcd(t, GDN_CHUNK)
    n = t // c

    def chunks(a):
        a = a.reshape((b, n, c, h) + a.shape[3:])
        return jnp.moveaxis(jnp.moveaxis(a, 1, 0), 3, 2)

    qc, kc, vc, gcum, bc = chunks(q), chunks(k), chunks(v), jnp.cumsum(chunks(g), axis=-1), chunks(beta)
    causal = jnp.tril(jnp.ones((c, c), dtype=bool))
    strict = jnp.tril(jnp.ones((c, c), dtype=bool), -1)
    decay = jnp.exp(jnp.where(causal, gcum[..., :, None] - gcum[..., None, :], -jnp.inf))
    kb = kc * bc[..., None]
    lower = jnp.where(strict, jnp.einsum('nbhid,nbhjd->nbhij', kb, kc) * decay, 0.0)
    a_mat = lower + jnp.eye(c, dtype=lower.dtype)
    rhs = jnp.concatenate([vc * bc[..., None], kb * jnp.exp(gcum)[..., None]], axis=-1)
    sol = lax.linalg.triangular_solve(a_mat, rhs, left_side=True, lower=True, unit_diagonal=True)
    uc, wc = sol[..., :dv], sol[..., dv:]
    qk = jnp.where(causal, jnp.einsum('nbhid,nbhjd->nbhij', qc, kc) * decay, 0.0)

    def step(s, inp):
        q_i, k_i, u_i, w_i, qk_i, g_i = inp
        v_new = u_i - jnp.einsum('bhcd,bhde->bhce', w_i, s)
        o = jnp.einsum('bhcd,bhde->bhce', q_i * jnp.exp(g_i)[..., None], s) + jnp.einsum('bhij,bhje->bhie', qk_i, v_new)
        g_last = g_i[..., -1:]
        s = s * jnp.exp(g_last)[..., None] + jnp.einsum('bhcd,bhce->bhde', k_i * jnp.exp(g_last - g_i)[..., None], v_new)
        return s, o

    s_new, o = lax.scan(step, s0, (qc, kc, uc, wc, qk, gcum))
    o = jnp.moveaxis(jnp.moveaxis(o, 2, 3), 0, 1).reshape(b, t, h, dv)
    return o, s_new


def gdn_mixer(u, s0, conv0, conv_w, a_log, dt_bias, norm_g):
    b, t, _ = u.shape
    qkv, z, beta_logit, a_logit = jnp.split(u, [GDN_CONV_DIM, GDN_CONV_DIM + GDN_WIDTH, GDN_CONV_DIM + GDN_WIDTH + GDN_HEADS], axis=-1)
    qkv_c, conv_new = causal_conv(qkv, conv0, conv_w)
    qkv_c = jax.nn.silu(qkv_c)
    q, k, v = jnp.split(qkv_c, [GDN_HEADS * GDN_DK, 2 * GDN_HEADS * GDN_DK], axis=-1)
    q = l2_normalize(q.reshape(b, t, GDN_HEADS, GDN_DK)) * (GDN_DK ** -0.5)
    k = l2_normalize(k.reshape(b, t, GDN_HEADS, GDN_DK))
    v = v.reshape(b, t, GDN_HEADS, GDN_DV).astype(jnp.float32)
    beta = jax.nn.sigmoid(beta_logit.astype(jnp.float32))
    g = -jnp.exp(a_log.astype(jnp.float32)) * jax.nn.softplus((a_logit + dt_bias).astype(jnp.float32))
    o, s_new = chunked_gated_delta(q, k, v, g, beta, s0.astype(jnp.float32))
    o = rms_norm(o.astype(u.dtype), norm_g) * jax.nn.silu(z.reshape(b, t, GDN_HEADS, GDN_DV))
    return o.reshape(b, t, GDN_WIDTH), s_new.astype(s0.dtype), conv_new


def trunk_layer(x, c, past, lp):
    (past_lat, past_rope, past_k, past_v, past_logf, lru_h0, lru_conv0, gdn_s0, gdn_conv0) = past
    (w_ada, b_ada, norm_mix_g, w_in, mla_q_a_g, mla_w_uq, mla_q_norm_g, mla_kv_norm_g, mla_kpe_norm_g,
     mla_w_uk, mla_w_uv, fox_q_norm_g, fox_k_norm_g, fox_f_bias, lru_conv_w, lru_conv_b, lru_w_a, lru_b_a,
     lru_w_x, lru_b_x, lru_lambda, gdn_conv_w, gdn_a_log, gdn_dt_bias, gdn_norm_g, w_out, norm_ffn_g,
     w_gate, w_up, w_down) = lp
    mod = jnp.einsum('bd,de->be', jax.nn.silu(c), w_ada) + b_ada
    shift1, scale1, gate1, shift2, scale2, gate2 = jnp.split(mod[:, None, :], 6, axis=-1)
    h = rms_norm(x, norm_mix_g) * (1.0 + scale1) + shift1
    u = jnp.einsum('btd,de->bte', h, w_in)
    u_mla, u_fox, u_lru, u_gdn = jnp.split(u, [IN_MLA, IN_MLA + IN_FOX, IN_MLA + IN_FOX + IN_LRU], axis=-1)
    o_mla, lat_new, rope_new = mla_mixer(u_mla, past_lat, past_rope, mla_q_a_g, mla_w_uq, mla_q_norm_g,
                                         mla_kv_norm_g, mla_kpe_norm_g, mla_w_uk, mla_w_uv)
    o_fox, k_new, v_new, logf_new = fox_mixer(u_fox, past_k, past_v, past_logf, fox_q_norm_g, fox_k_norm_g, fox_f_bias)
    o_lru, h_new, lru_conv_new = lru_mixer(u_lru, lru_h0, lru_conv0, lru_conv_w, lru_conv_b, lru_w_a, lru_b_a,
                                           lru_w_x, lru_b_x, lru_lambda)
    o_gdn, s_new, gdn_conv_new = gdn_mixer(u_gdn, gdn_s0, gdn_conv0, gdn_conv_w, gdn_a_log, gdn_dt_bias, gdn_norm_g)
    mix = jnp.concatenate([o_mla, o_fox, o_lru, o_gdn], axis=-1)
    x = x + gate1 * jnp.einsum('bte,ed->btd', mix, w_out)
    h = rms_norm(x, norm_ffn_g) * (1.0 + scale2) + shift2
    ff = jnp.einsum('btf,fd->btd', jax.nn.silu(jnp.einsum('btd,df->btf', h, w_gate)) * jnp.einsum('btd,df->btf', h, w_up), w_down)
    x = x + gate2 * ff
    return x, (lat_new, rope_new, k_new, v_new, logf_new, h_new, lru_conv_new, s_new, gdn_conv_new)


def gather_pages(cache, layer, page_table):
    rows = cache[layer, page_table]
    return rows.reshape((rows.shape[0], rows.shape[1] * rows.shape[2]) + rows.shape[3:])


def stack_states(states):
    return tuple(jnp.stack(arrs, axis=0) for arrs in zip(*states))


def setup_inputs(seed: int = 0) -> dict:
    key = jax.random.key(seed)
    keys = iter(jax.random.split(key, 64))

    def nrm(shape, scale=1.0):
        return scale * jax.random.normal(next(keys), shape, jnp.float32)

    def gain(shape):
        return 1.0 + 0.1 * jax.random.normal(next(keys), shape, jnp.float32)

    def unif(shape, lo, hi):
        return jax.random.uniform(next(keys), shape, jnp.float32, lo, hi)

    n_pages = PAST_LEN // PAGE_SIZE
    n_used = DEC_BATCH * n_pages
    n_pool = n_used + n_used // 4
    bw = LRU_WIDTH // LRU_BLOCKS
    a0 = unif((DEPTH, LRU_WIDTH), 0.9, 0.999)
    dt0 = jnp.exp(unif((DEPTH, GDN_HEADS), math.log(1e-3), math.log(1e-1)))
    return {
        'x_prompt': nrm((BATCH, SEQ, D_MODEL)),
        'x_sample': nrm((DEC_BATCH, DEC_SEQ, D_MODEL)),
        'cache_mla_lat': nrm((DEPTH, n_pool, PAGE_SIZE, MLA_KV_LORA)),
        'cache_mla_rope': nrm((DEPTH, n_pool, PAGE_SIZE, MLA_ROPE)),
        'cache_fox_k': nrm((DEPTH, n_pool, PAGE_SIZE, FOX_KV_HEADS, FOX_HD)),
        'cache_fox_v': nrm((DEPTH, n_pool, PAGE_SIZE, FOX_KV_HEADS, FOX_HD)),
        'cache_fox_logf': jax.nn.log_sigmoid(2.5 + nrm((DEPTH, n_pool, PAGE_SIZE, FOX_HEADS))),
        'state_lru_h': nrm((DEPTH, DEC_BATCH, LRU_WIDTH), 0.5),
        'state_lru_conv': nrm((DEPTH, DEC_BATCH, CONV_W - 1, LRU_WIDTH)),
        'state_gdn_S': nrm((DEPTH, DEC_BATCH, GDN_HEADS, GDN_DK, GDN_DV), 0.1),
        'state_gdn_conv': nrm((DEPTH, DEC_BATCH, CONV_W - 1, GDN_CONV_DIM)),
        'page_table': jax.random.permutation(next(keys), n_pool)[:n_used].reshape(DEC_BATCH, n_pages).astype(jnp.int32),
        'c_prompt': nrm((BATCH, D_MODEL)),
        'c_sample': nrm((DEC_BATCH, D_MODEL)),
        'w_ada': nrm((DEPTH, D_MODEL, 6 * D_MODEL), 0.5 * D_MODEL ** -0.5),
        'b_ada': nrm((DEPTH, 6 * D_MODEL), 0.02),
        'norm_mix_g': gain((DEPTH, D_MODEL)),
        'w_in': nrm((DEPTH, D_MODEL, D_IN), D_MODEL ** -0.5),
        'mla_q_a_g': gain((DEPTH, MLA_Q_LORA)),
        'mla_w_uq': nrm((DEPTH, MLA_Q_LORA, MLA_HEADS, MLA_NOPE + MLA_ROPE), MLA_Q_LORA ** -0.5),
        'mla_q_norm_g': gain((DEPTH, MLA_NOPE + MLA_ROPE)),
        'mla_kv_norm_g': gain((DEPTH, MLA_KV_LORA)),
        'mla_kpe_norm_g': gain((DEPTH, MLA_ROPE)),
        'mla_w_uk': nrm((DEPTH, MLA_KV_LORA, MLA_HEADS, MLA_NOPE), MLA_KV_LORA ** -0.5),
        'mla_w_uv': nrm((DEPTH, MLA_KV_LORA, MLA_HEADS, MLA_V), MLA_KV_LORA ** -0.5),
        'fox_q_norm_g': gain((DEPTH, FOX_HD)),
        'fox_k_norm_g': gain((DEPTH, FOX_HD)),
        'fox_f_bias': unif((DEPTH, FOX_HEADS), 1.0, 4.0),
        'lru_conv_w': nrm((DEPTH, CONV_W, LRU_WIDTH), 0.5),
        'lru_conv_b': nrm((DEPTH, LRU_WIDTH), 0.02),
        'lru_w_a': nrm((DEPTH, LRU_BLOCKS, bw, bw), bw ** -0.5),
        'lru_b_a': nrm((DEPTH, LRU_WIDTH), 0.02),
        'lru_w_x': nrm((DEPTH, LRU_BLOCKS, bw, bw), bw ** -0.5),
        'lru_b_x': nrm((DEPTH, LRU_WIDTH), 0.02),
        'lru_lambda': jnp.log(a0) - jnp.log1p(-a0),
        'gdn_conv_w': nrm((DEPTH, CONV_W, GDN_CONV_DIM), 0.5),
        'gdn_a_log': jnp.log(unif((DEPTH, GDN_HEADS), 1.0, 16.0)),
        'gdn_dt_bias': dt0 + jnp.log(-jnp.expm1(-dt0)),
        'gdn_norm_g': gain((DEPTH, GDN_DV)),
        'w_out': nrm((DEPTH, D_MIX, D_MODEL), D_MIX ** -0.5),
        'norm_ffn_g': gain((DEPTH, D_MODEL)),
        'w_gate': nrm((DEPTH, D_MODEL, D_FF), D_MODEL ** -0.5),
        'w_up': nrm((DEPTH, D_MODEL, D_FF), D_MODEL ** -0.5),
        'w_down': nrm((DEPTH, D_FF, D_MODEL), D_FF ** -0.5),
    }


def reference(x_prompt, x_sample, cache_mla_lat, cache_mla_rope, cache_fox_k, cache_fox_v, cache_fox_logf,
              state_lru_h, state_lru_conv, state_gdn_S, state_gdn_conv, page_table, c_prompt, c_sample,
              w_ada, b_ada, norm_mix_g, w_in, mla_q_a_g, mla_w_uq, mla_q_norm_g, mla_kv_norm_g, mla_kpe_norm_g,
              mla_w_uk, mla_w_uv, fox_q_norm_g, fox_k_norm_g, fox_f_bias, lru_conv_w, lru_conv_b, lru_w_a,
              lru_b_a, lru_w_x, lru_b_x, lru_lambda, gdn_conv_w, gdn_a_log, gdn_dt_bias, gdn_norm_g, w_out,
              norm_ffn_g, w_gate, w_up, w_down):
    layer_params = (w_ada, b_ada, norm_mix_g, w_in, mla_q_a_g, mla_w_uq, mla_q_norm_g, mla_kv_norm_g,
                    mla_kpe_norm_g, mla_w_uk, mla_w_uv, fox_q_norm_g, fox_k_norm_g, fox_f_bias, lru_conv_w,
                    lru_conv_b, lru_w_a, lru_b_a, lru_w_x, lru_b_x, lru_lambda, gdn_conv_w, gdn_a_log,
                    gdn_dt_bias, gdn_norm_g, w_out, norm_ffn_g, w_gate, w_up, w_down)
    dt = x_prompt.dtype
    bp = x_prompt.shape[0]
    prompt_past = (jnp.zeros((bp, 0, MLA_KV_LORA), dt), jnp.zeros((bp, 0, MLA_ROPE), dt),
                   jnp.zeros((bp, 0, FOX_KV_HEADS, FOX_HD), dt), jnp.zeros((bp, 0, FOX_KV_HEADS, FOX_HD), dt),
                   jnp.zeros((bp, 0, FOX_HEADS), dt), jnp.zeros((bp, LRU_WIDTH), dt),
                   jnp.zeros((bp, CONV_W - 1, LRU_WIDTH), dt), jnp.zeros((bp, GDN_HEADS, GDN_DK, GDN_DV), dt),
                   jnp.zeros((bp, CONV_W - 1, GDN_CONV_DIM), dt))
    x_p, x_s = x_prompt, x_sample
    states_p, states_s = [], []
    for l in range(DEPTH):
        lp = tuple(p[l] for p in layer_params)
        x_p, st_p = trunk_layer(x_p, c_prompt, prompt_past, lp)
        states_p.append(st_p)
        sample_past = (gather_pages(cache_mla_lat, l, page_table), gather_pages(cache_mla_rope, l, page_table),
                       gather_pages(cache_fox_k, l, page_table), gather_pages(cache_fox_v, l, page_table),
                       gather_pages(cache_fox_logf, l, page_table), state_lru_h[l], state_lru_conv[l],
                       state_gdn_S[l], state_gdn_conv[l])
        x_s, st_s = trunk_layer(x_s, c_sample, sample_past, lp)
        states_s.append(st_s)
    (mla_lat_p, mla_rope_p, fox_k_p, fox_v_p, fox_logf_p, lru_h_p, lru_conv_p, gdn_S_p, gdn_conv_p) = stack_states(states_p)
    (mla_lat_s, mla_rope_s, fox_k_s, fox_v_s, fox_logf_s, lru_h_s, lru_conv_s, gdn_S_s, gdn_conv_s) = stack_states(states_s)
    return (x_p, x_s, mla_lat_p, mla_lat_s, mla_rope_p, mla_rope_s, fox_k_p, fox_k_s, fox_v_p, fox_v_s,
            fox_logf_p, fox_logf_s, lru_h_p, lru_h_s, lru_conv_p, lru_conv_s, gdn_S_p, gdn_S_s,
            gdn_conv_p, gdn_conv_s)
```

```python
import functools
import math

import jax
import jax.numpy as jnp
from jax import lax
from jax.experimental import pallas as pl
from jax.experimental.pallas import tpu as pltpu

F32 = jnp.float32
BF16 = jnp.bfloat16
HI = lax.Precision.HIGHEST

D_MODEL = 2048
DEPTH = 2
PAGE = 128
MLA_HEADS = 4
MLA_Q_LORA = 384
MLA_KV_LORA = 128
MLA_NOPE = 128
MLA_ROPE = 64
ROPE_THETA = 10000.0
FOX_HEADS = 4
FOX_HD = 128
LRU_WIDTH = 512
LRU_BLOCKS = 4
LRU_C = 8.0
GDN_HEADS = 4
GDN_DK = 128
GDN_DV = 128
GDN_CHUNK = 64
CONV_W = 4
NORM_EPS = 1e-6
LANE = 128
SUBLANE = 8
VMEM_LIMIT = 56 * 1024 * 1024
PAGES_PER_STEP = 16

C_QA, C_KVA, C_KPE = 0, 384, 1024
C_FQ, C_FK, C_FV, C_FL = 512, 1152, 1280, 1408
C_LX, C_LG = 1536, 2048
C_GQ, C_GZ, C_GB = 2560, 4096, 4608
D_INP = 5120


def _cp(sem):
    return pltpu.CompilerParams(dimension_semantics=sem, vmem_limit_bytes=VMEM_LIMIT)


def _sigmoid(x):
    return 1.0 / (1.0 + jnp.exp(-x))


def _softplus(x):
    return jnp.maximum(x, 0.0) + jnp.log1p(jnp.exp(-jnp.abs(x)))


def _dot(a, b):
    return jnp.dot(a.astype(BF16), b.astype(BF16), preferred_element_type=F32)


def _dot_nt(a, b):
    return lax.dot_general(a.astype(BF16), b.astype(BF16), (((1,), (1,)), ((), ())), preferred_element_type=F32)


def _dot_hi(a, b):
    return jnp.dot(a, b, precision=HI, preferred_element_type=F32)


def _rms(x, g, n=None):
    n = x.shape[-1] if n is None else n
    ms = jnp.sum(x * x, axis=-1, keepdims=True) * (1.0 / n)
    return x * lax.rsqrt(ms + NORM_EPS) * g


def _ada_kernel(c_ref, w_ref, b_ref, o_ref):
    c = c_ref[...]
    o_ref[...] = _dot(c * _sigmoid(c), w_ref[...]) + b_ref[...]


def ada_mod(c_all, w_ada, b_ada, layer):
    m, d = c_all.shape
    n = w_ada.shape[-1]
    tn = 1024
    return pl.pallas_call(
        _ada_kernel,
        grid=(n // tn,),
        in_specs=[pl.BlockSpec((m, d), lambda j: (0, 0)),
                  pl.BlockSpec((None, d, tn), lambda j: (layer, 0, j)),
                  pl.BlockSpec((None, 1, tn), lambda j: (layer, 0, j))],
        out_specs=pl.BlockSpec((m, tn), lambda j: (0, j)),
        out_shape=jax.ShapeDtypeStruct((m, n), F32),
        compiler_params=_cp(("arbitrary",)),
        name="ada_mod",
    )(c_all, w_ada, b_ada.reshape(DEPTH, 1, n))


def _mod_spec(mod, tm, rows_per_group):
    if mod.ndim == 3:
        bpg = rows_per_group // tm
        return pl.BlockSpec((None, 1, mod.shape[-1]), lambda i, j: (i // bpg, 0, 0))
    return pl.BlockSpec((tm, mod.shape[-1]), lambda i, j: (i, 0))


def _norm_mod(x_ref, g_ref, sc_ref, sh_ref, h_ref):
    x = x_ref[...]
    y = x * lax.rsqrt(jnp.mean(x * x, axis=-1, keepdims=True) + NORM_EPS) * g_ref[...]
    h_ref[...] = (y * (1.0 + sc_ref[...]) + sh_ref[...]).astype(BF16)


def _nmm_kernel(x_ref, g_ref, sc_ref, sh_ref, w_ref, o_ref, h_ref):
    @pl.when(pl.program_id(1) == 0)
    def _():
        _norm_mod(x_ref, g_ref, sc_ref, sh_ref, h_ref)
    o_ref[...] = jnp.dot(h_ref[...], w_ref[...], preferred_element_type=F32)


def norm_mod_matmul(x, g, scale, shift, w, rows_per_group, tm, tn):
    m, d = x.shape
    n = w.shape[1]
    return pl.pallas_call(
        _nmm_kernel,
        grid=(m // tm, n // tn),
        in_specs=[pl.BlockSpec((tm, d), lambda i, j: (i, 0)),
                  pl.BlockSpec((1, d), lambda i, j: (0, 0)),
                  _mod_spec(scale, tm, rows_per_group),
                  _mod_spec(shift, tm, rows_per_group),
                  pl.BlockSpec((d, tn), lambda i, j: (0, j))],
        out_specs=pl.BlockSpec((tm, tn), lambda i, j: (i, j)),
        out_shape=jax.ShapeDtypeStruct((m, n), F32),
        scratch_shapes=[pltpu.VMEM((tm, d), BF16)],
        compiler_params=_cp(("parallel", "arbitrary")),
        name="norm_mod_matmul",
    )(x, g.reshape(1, d), scale, shift, w)


def _ffn_up_kernel(x_ref, g_ref, sc_ref, sh_ref, wg_ref, wu_ref, o_ref, h_ref):
    @pl.when(pl.program_id(1) == 0)
    def _():
        _norm_mod(x_ref, g_ref, sc_ref, sh_ref, h_ref)
    h = h_ref[...]
    a = jnp.dot(h, wg_ref[...], preferred_element_type=F32)
    b = jnp.dot(h, wu_ref[...], preferred_element_type=F32)
    o_ref[...] = (a * _sigmoid(a) * b).astype(BF16)


def ffn_up(x, g, scale, shift, wg, wu, rows_per_group, tm, tn):
    m, d = x.shape
    n = wg.shape[1]
    return pl.pallas_call(
        _ffn_up_kernel,
        grid=(m // tm, n // tn),
        in_specs=[pl.BlockSpec((tm, d), lambda i, j: (i, 0)),
                  pl.BlockSpec((1, d), lambda i, j: (0, 0)),
                  _mod_spec(scale, tm, rows_per_group),
                  _mod_spec(shift, tm, rows_per_group),
                  pl.BlockSpec((d, tn), lambda i, j: (0, j)),
                  pl.BlockSpec((d, tn), lambda i, j: (0, j))],
        out_specs=pl.BlockSpec((tm, tn), lambda i, j: (i, j)),
        out_shape=jax.ShapeDtypeStruct((m, n), BF16),
        scratch_shapes=[pltpu.VMEM((tm, d), BF16)],
        compiler_params=_cp(("parallel", "arbitrary")),
        name="ffn_up",
    )(x, g.reshape(1, d), scale, shift, wg, wu)


def _mm_res_kernel(*refs, n_in):
    a_refs, w_refs = refs[:n_in], refs[n_in:2 * n_in]
    x_ref, gate_ref, o_ref = refs[2 * n_in:]
    acc = _dot(a_refs[0][...], w_refs[0][...])
    for a_ref, w_ref in zip(a_refs[1:], w_refs[1:]):
        acc = acc + _dot(a_ref[...], w_ref[...])
    o_ref[...] = x_ref[...] + gate_ref[...] * acc


def matmul_residual(a_list, w, x, gate, rows_per_group, tm, tn):
    m, d = x.shape
    n_in = len(a_list)
    kk = a_list[0].shape[1]
    in_specs = [pl.BlockSpec((tm, kk), lambda i, j: (i, 0)) for _ in a_list]
    in_specs += [pl.BlockSpec((kk, tn), lambda i, j, k=k: (k, j)) for k in range(n_in)]
    in_specs += [pl.BlockSpec((tm, tn), lambda i, j: (i, j))]
    if gate.ndim == 3:
        bpg = rows_per_group // tm
        in_specs += [pl.BlockSpec((None, 1, tn), lambda i, j: (i // bpg, 0, j))]
    else:
        in_specs += [pl.BlockSpec((tm, tn), lambda i, j: (i, j))]
    return pl.pallas_call(
        functools.partial(_mm_res_kernel, n_in=n_in),
        grid=(m // tm, d // tn),
        in_specs=in_specs,
        out_specs=pl.BlockSpec((tm, tn), lambda i, j: (i, j)),
        out_shape=jax.ShapeDtypeStruct((m, d), F32),
        compiler_params=_cp(("parallel", "arbitrary")),
        name="matmul_residual",
    )(*a_list, *([w] * n_in), x, gate)


def rope_tables(pos):
    half = MLA_ROPE // 2
    freqs = ROPE_THETA ** (-jnp.arange(half, dtype=F32) / half)
    ang = pos.astype(F32)[:, None] * freqs[None, :]
    cos, sin = jnp.cos(ang), jnp.sin(ang)
    z = jnp.zeros_like(cos)
    tc = jnp.concatenate([cos, cos, z, z], axis=-1)
    ts1 = jnp.concatenate([-sin, z, z, z], axis=-1)
    ts2 = jnp.concatenate([z, sin, z, z], axis=-1)
    return tc, ts1, ts2


def _rope(x, tc, ts1, ts2):
    return x * tc + pltpu.roll(x, LANE - MLA_ROPE // 2, 1) * ts1 + pltpu.roll(x, MLA_ROPE // 2, 1) * ts2


def _mla_pre_kernel(qa_ref, kva_ref, kpe_ref, tc_ref, ts1_ref, ts2_ref, gqa_ref, wuq_ref, gqn_ref, gqr_ref,
                    gkv_ref, gkpe_ref, wuk_ref, q_ref, lat_ref, rot_ref, kf_ref):
    tc, ts1, ts2 = tc_ref[...], ts1_ref[...], ts2_ref[...]
    qn = _rms(qa_ref[...], gqa_ref[...])
    q = _dot(qn, wuq_ref[...])
    hw = MLA_HEADS * MLA_NOPE
    for h in range(MLA_HEADS):
        nope = q[:, h * LANE:(h + 1) * LANE]
        rope = q[:, hw + h * LANE:hw + (h + 1) * LANE]
        ms = (jnp.sum(nope * nope, axis=-1, keepdims=True) + jnp.sum(rope * rope, axis=-1, keepdims=True)) * (
            1.0 / (MLA_NOPE + MLA_ROPE))
        r = lax.rsqrt(ms + NORM_EPS)
        q_ref[h, :, 0:LANE] = _dot(nope * r * gqn_ref[...], wuk_ref[h])
        q_ref[h, :, LANE:2 * LANE] = _rope(rope * r * gqr_ref[...], tc, ts1, ts2)
    lat = _rms(kva_ref[...], gkv_ref[...])
    rot = _rope(_rms(kpe_ref[...], gkpe_ref[...], MLA_ROPE), tc, ts1, ts2)
    lat_ref[...] = lat
    rot_ref[...] = rot[:, :MLA_ROPE]
    kf_ref[:, 0:LANE] = lat.astype(BF16)
    kf_ref[:, LANE:2 * LANE] = rot.astype(BF16)


def mla_pre(u, tabs, tab_blocks, gqa, wuq, gqn, gqr, gkv, gkpe, wuk, tm):
    m = u.shape[0]
    col = lambda c, w: pl.BlockSpec((tm, w), lambda i: (i, c // w))
    tab = pl.BlockSpec((tm, LANE), lambda i: (i % tab_blocks, 0))
    full = lambda a: pl.BlockSpec(a.shape, lambda i: (0,) * a.ndim)
    return pl.pallas_call(
        _mla_pre_kernel,
        grid=(m // tm,),
        in_specs=[col(C_QA, MLA_Q_LORA), col(C_KVA, LANE), col(C_KPE, LANE), tab, tab, tab,
                  full(gqa), full(wuq), full(gqn), full(gqr), full(gkv), full(gkpe), full(wuk)],
        out_specs=[pl.BlockSpec((MLA_HEADS, tm, 2 * LANE), lambda i: (0, i, 0)),
                   pl.BlockSpec((tm, LANE), lambda i: (i, 0)),
                   pl.BlockSpec((tm, MLA_ROPE), lambda i: (i, 0)),
                   pl.BlockSpec((tm, 2 * LANE), lambda i: (i, 0))],
        out_shape=[jax.ShapeDtypeStruct((MLA_HEADS, m, 2 * LANE), F32),
                   jax.ShapeDtypeStruct((m, LANE), F32),
                   jax.ShapeDtypeStruct((m, MLA_ROPE), F32),
                   jax.ShapeDtypeStruct((m, 2 * LANE), BF16)],
        compiler_params=_cp(("parallel",)),
        name="mla_pre",
    )(u, u, u, *tabs, gqa, wuq, gqn, gqr, gkv, gkpe, wuk)


def _head_proj_kernel(o_ref, w_ref, y_ref):
    for h in range(MLA_HEADS):
        y_ref[:, h * LANE:(h + 1) * LANE] = _dot(o_ref[h], w_ref[h])


def head_proj(o, w, tm):
    m = o.shape[1]
    return pl.pallas_call(
        _head_proj_kernel,
        grid=(m // tm,),
        in_specs=[pl.BlockSpec((MLA_HEADS, tm, LANE), lambda i: (0, i, 0)),
                  pl.BlockSpec(w.shape, lambda i: (0, 0, 0))],
        out_specs=pl.BlockSpec((tm, MLA_HEADS * LANE), lambda i: (i, 0)),
        out_shape=jax.ShapeDtypeStruct((m, MLA_HEADS * LANE), F32),
        compiler_params=_cp(("parallel",)),
        name="head_proj",
    )(o, w)


def _fox_pre_kernel(q_ref, k_ref, v_ref, f_ref, gq_ref, gk_ref, fb_ref, qo_ref, ko_ref, vo_ref, lf_ref, *rest, with_cum):
    for h in range(FOX_HEADS):
        qo_ref[h] = _rms(q_ref[:, h * LANE:(h + 1) * LANE], gq_ref[...])
    ko_ref[...] = _rms(k_ref[...], gk_ref[...])
    vo_ref[...] = v_ref[...]
    z = f_ref[...] + fb_ref[...]
    logf = jnp.minimum(z, 0.0) - jnp.log1p(jnp.exp(-jnp.abs(z)))
    lane = lax.broadcasted_iota(jnp.int32, logf.shape, 1)
    logf = jnp.where(lane < FOX_HEADS, logf, 0.0)
    lf_ref[...] = logf[:, :FOX_HEADS]
    if with_cum:
        cc_ref, cr_ref, carry_ref = rest
        tm = logf.shape[0]

        @pl.when(pl.program_id(1) == 0)
        def _():
            carry_ref[...] = jnp.zeros_like(carry_ref)
        tri = (lax.broadcasted_iota(jnp.int32, (tm, tm), 0) >= lax.broadcasted_iota(jnp.int32, (tm, tm), 1)).astype(F32)
        cum = _dot_hi(tri, logf) + carry_ref[...]
        carry_ref[...] = cum[tm - 1:tm, :]
        cc_ref[...] = cum[:, :FOX_HEADS]
        cr_ref[...] = cum.T[:FOX_HEADS, :]


def fox_pre(u, gq, gk, fb, nb, tm, with_cum):
    m = u.shape[0]
    t = m // nb
    nt = t // tm
    col = lambda c, w: pl.BlockSpec((tm, w), lambda b, i: (b * nt + i, c // w))
    full = lambda a: pl.BlockSpec(a.shape, lambda b, i: (0,) * a.ndim)
    row = pl.BlockSpec((tm, LANE), lambda b, i: (b * nt + i, 0))
    out_specs = [pl.BlockSpec((FOX_HEADS, tm, LANE), lambda b, i: (0, b * nt + i, 0)), row, row,
                 pl.BlockSpec((tm, FOX_HEADS), lambda b, i: (b * nt + i, 0))]
    out_shape = [jax.ShapeDtypeStruct((FOX_HEADS, m, LANE), F32), jax.ShapeDtypeStruct((m, LANE), F32),
                 jax.ShapeDtypeStruct((m, LANE), F32), jax.ShapeDtypeStruct((m, FOX_HEADS), F32)]
    scratch = []
    if with_cum:
        out_specs += [pl.BlockSpec((tm, FOX_HEADS), lambda b, i: (b * nt + i, 0)),
                      pl.BlockSpec((None, FOX_HEADS, tm), lambda b, i: (b, 0, i))]
        out_shape += [jax.ShapeDtypeStruct((m, FOX_HEADS), F32), jax.ShapeDtypeStruct((nb, FOX_HEADS, t), F32)]
        scratch = [pltpu.VMEM((1, LANE), F32)]
    return pl.pallas_call(
        functools.partial(_fox_pre_kernel, with_cum=with_cum),
        grid=(nb, nt),
        in_specs=[col(C_FQ, FOX_HEADS * FOX_HD), col(C_FK, LANE), col(C_FV, LANE), col(C_FL, LANE),
                  full(gq), full(gk), full(fb)],
        out_specs=out_specs,
        out_shape=out_shape,
        scratch_shapes=scratch,
        compiler_params=_cp(("parallel", "arbitrary")),
        name="fox_pre",
    )(u, u, u, u, gq, gk, fb)


def _causal_attn_kernel(*refs, scale, tq, tk, nheads, dv, has_bias, v_from_k, concat_heads):
    it = iter(refs)
    q_ref, k_ref = next(it), next(it)
    v_ref = k_ref if v_from_k else next(it)
    cc_ref, cr_ref = (next(it), next(it)) if has_bias else (None, None)
    o_ref, qs_ref, m_ref, l_ref, acc_ref = next(it), next(it), next(it), next(it), next(it)
    qi, ki = pl.program_id(1), pl.program_id(2)

    @pl.when(ki == 0)
    def _():
        m_ref[...] = jnp.full(m_ref.shape, -jnp.inf, F32)
        l_ref[...] = jnp.zeros_like(l_ref)
        acc_ref[...] = jnp.zeros_like(acc_ref)
        qs_ref[...] = q_ref[...].astype(BF16)

    @pl.when(ki * tk <= qi * tq + tq - 1)
    def _():
        k = k_ref[...].astype(BF16)
        v = v_ref[:, 0:dv].astype(BF16)
        row = qi * tq + lax.broadcasted_iota(jnp.int32, (tq, tk), 0)
        colk = ki * tk + lax.broadcasted_iota(jnp.int32, (tq, tk), 1)
        visible = colk <= row
        for h in range(nheads):
            s = _dot_nt(qs_ref[h], k) * scale
            if has_bias:
                s = s + (cc_ref[:, h:h + 1] - cr_ref[h:h + 1, :])
            s = jnp.where(visible, s, -jnp.inf)
            m_prev = m_ref[h]
            m_new = jnp.maximum(m_prev, jnp.max(s, axis=-1, keepdims=True))
            alpha = jnp.exp(m_prev - m_new)
            p = jnp.exp(s - m_new)
            l_ref[h] = alpha * l_ref[h] + jnp.sum(p, axis=-1, keepdims=True)
            acc_ref[h] = alpha * acc_ref[h] + jnp.dot(p.astype(BF16), v, preferred_element_type=F32)
            m_ref[h] = m_new

    @pl.when(ki == pl.num_programs(2) - 1)
    def _():
        for h in range(nheads):
            o = acc_ref[h] / l_ref[h]
            if concat_heads:
                o_ref[:, h * dv:(h + 1) * dv] = o
            else:
                o_ref[h] = o


def causal_attn(q, k, v, cum_col, cum_row, nb, scale, dv, tq, tk, concat_heads):
    nheads, m, dq = q.shape
    t = m // nb
    nq, nk = t // tq, t // tk
    kclamp = lambda qi, ki: jnp.minimum(ki, (qi * tq + tq - 1) // tk)
    in_specs = [pl.BlockSpec((nheads, tq, dq), lambda b, qi, ki: (0, b * nq + qi, 0)),
                pl.BlockSpec((tk, k.shape[1]), lambda b, qi, ki: (b * nk + kclamp(qi, ki), 0))]
    args = [q, k]
    if v is not None:
        in_specs.append(pl.BlockSpec((tk, v.shape[1]), lambda b, qi, ki: (b * nk + kclamp(qi, ki), 0)))
        args.append(v)
    if cum_col is not None:
        in_specs += [pl.BlockSpec((tq, nheads), lambda b, qi, ki: (b * nq + qi, 0)),
                     pl.BlockSpec((None, nheads, tk), lambda b, qi, ki: (b, 0, kclamp(qi, ki)))]
        args += [cum_col, cum_row]
    if concat_heads:
        out_spec = pl.BlockSpec((tq, nheads * dv), lambda b, qi, ki: (b * nq + qi, 0))
        out_shape = jax.ShapeDtypeStruct((m, nheads * dv), F32)
    else:
        out_spec = pl.BlockSpec((nheads, tq, dv), lambda b, qi, ki: (0, b * nq + qi, 0))
        out_shape = jax.ShapeDtypeStruct((nheads, m, dv), F32)
    return pl.pallas_call(
        functools.partial(_causal_attn_kernel, scale=scale, tq=tq, tk=tk, nheads=nheads, dv=dv,
                          has_bias=cum_col is not None, v_from_k=v is None, concat_heads=concat_heads),
        grid=(nb, nq, nk),
        in_specs=in_specs,
        out_specs=out_spec,
        out_shape=out_shape,
        scratch_shapes=[pltpu.VMEM((nheads, tq, dq), BF16), pltpu.VMEM((nheads, tq, 1), F32),
                        pltpu.VMEM((nheads, tq, 1), F32), pltpu.VMEM((nheads, tq, dv), F32)],
        compiler_params=_cp(("parallel", "parallel", "arbitrary")),
        name="causal_attn",
    )(*args)


def _causal_conv(x_ref, c0_ref, w_ref, xbuf_ref, cnew_ref, tt):
    lo = SUBLANE - (CONV_W - 1)

    @pl.when(pl.program_id(1) == 0)
    def _():
        xbuf_ref[:, lo:SUBLANE, :] = c0_ref[...]
    xbuf_ref[:, SUBLANE:SUBLANE + tt, :] = x_ref[...]
    y = xbuf_ref[:, lo:lo + tt, :] * w_ref[0:1, :]
    for j in range(1, CONV_W):
        y = y + xbuf_ref[:, lo + j:lo + j + tt, :] * w_ref[j:j + 1, :]
    tail = xbuf_ref[:, lo + tt:SUBLANE + tt, :]
    cnew_ref[...] = tail
    xbuf_ref[:, lo:SUBLANE, :] = tail
    return y


def _lru_kernel(x_ref, gate_ref, c0_ref, h0_ref, cw_ref, cb_ref, wa_ref, ba_ref, wx_ref, bx_ref, lam_ref,
                y_ref, hl_ref, cnew_ref, xbuf_ref, hcar_ref, abuf_ref, bbuf_ref, *, tt):
    bb = x_ref.shape[0]

    @pl.when(pl.program_id(1) == 0)
    def _():
        hcar_ref[...] = h0_ref[...]
        abuf_ref[:, 0:tt, :] = jnp.ones((bb, tt, LRU_WIDTH), F32)
        bbuf_ref[:, 0:tt, :] = jnp.zeros((bb, tt, LRU_WIDTH), F32)

    xc = _causal_conv(x_ref, c0_ref, cw_ref, xbuf_ref, cnew_ref, tt) + cb_ref[...]
    xc2 = xc.reshape(bb * tt, LRU_WIDTH)
    bw = LRU_WIDTH // LRU_BLOCKS
    ra = jnp.concatenate([_dot(xc2[:, n * bw:(n + 1) * bw], wa_ref[n]) for n in range(LRU_BLOCKS)], axis=-1)
    rx = jnp.concatenate([_dot(xc2[:, n * bw:(n + 1) * bw], wx_ref[n]) for n in range(LRU_BLOCKS)], axis=-1)
    r = _sigmoid(ra + ba_ref[...])
    i = _sigmoid(rx + bx_ref[...])
    log_a = -LRU_C * r * _softplus(-lam_ref[...])
    a = jnp.exp(log_a).reshape(bb, tt, LRU_WIDTH)
    nem1 = -jnp.tanh(log_a) * (jnp.exp(2.0 * log_a) + 1.0)
    b = (jnp.sqrt(nem1) * (i * xc2)).reshape(bb, tt, LRU_WIDTH)
    s = 1
    while s < tt:
        abuf_ref[:, tt:2 * tt, :] = a
        bbuf_ref[:, tt:2 * tt, :] = b
        b = b + a * bbuf_ref[:, tt - s:2 * tt - s, :]
        a = a * abuf_ref[:, tt - s:2 * tt - s, :]
        s *= 2
    h = a * hcar_ref[...] + b
    hl = h[:, tt - 1:tt, :]
    hcar_ref[...] = hl
    hl_ref[...] = hl
    g = gate_ref[...]
    gelu = 0.5 * g * (1.0 + jnp.tanh(math.sqrt(2.0 / math.pi) * (g + 0.044715 * (g * g * g))))
    y_ref[...] = h * gelu


def lru_mixer(u3, h0, conv0, cw, cb, wa, ba, wx, bx, lam, bb, tt):
    nb, t, _ = u3.shape
    w = LRU_WIDTH
    full = lambda a: pl.BlockSpec(a.shape, lambda b, i: (0,) * a.ndim)
    blk = lambda c: pl.BlockSpec((bb, tt, w), lambda b, i: (b, i, c // w))
    st = lambda r: pl.BlockSpec((bb, r, w), lambda b, i: (b, 0, 0))
    return pl.pallas_call(
        functools.partial(_lru_kernel, tt=tt),
        grid=(nb // bb, t // tt),
        in_specs=[blk(C_LX), blk(C_LG), st(CONV_W - 1), st(1), full(cw), full(cb), full(wa), full(ba), full(wx),
                  full(bx), full(lam)],
        out_specs=[pl.BlockSpec((bb, tt, w), lambda b, i: (b, i, 0)), st(1), st(CONV_W - 1)],
        out_shape=[jax.ShapeDtypeStruct((nb, t, w), F32), jax.ShapeDtypeStruct((nb, 1, w), F32),
                   jax.ShapeDtypeStruct((nb, CONV_W - 1, w), F32)],
        scratch_shapes=[pltpu.VMEM((bb, SUBLANE + tt, w), F32), pltpu.VMEM((bb, 1, w), F32),
                        pltpu.VMEM((bb, 2 * tt, w), F32), pltpu.VMEM((bb, 2 * tt, w), F32)],
        compiler_params=_cp(("parallel", "arbitrary")),
        name="lru_mixer",
    )(u3, u3, conv0, h0, cw, cb, wa, ba, wx, bx, lam)


def _gdn_pre_kernel(xq_ref, xk_ref, xv_ref, gin_ref, cq_ref, ck_ref, cv_ref, wq_ref, wk_ref, wv_ref, alog_ref, dtb_ref,
                    q_ref, k_ref, v_ref, bg_ref, nq_ref, nk_ref, nv_ref, bq_ref, bk_ref, bv_ref, *, tt):
    groups = ((xq_ref, cq_ref, wq_ref, bq_ref, nq_ref, q_ref, GDN_DK ** -0.5),
              (xk_ref, ck_ref, wk_ref, bk_ref, nk_ref, k_ref, 1.0),
              (xv_ref, cv_ref, wv_ref, bv_ref, nv_ref, v_ref, None))
    for x_ref, c0_ref, w_ref, xbuf_ref, cnew_ref, o_ref, mult in groups:
        y = _causal_conv(x_ref, c0_ref, w_ref, xbuf_ref, cnew_ref, tt)
        y = y * _sigmoid(y)
        if mult is None:
            o_ref[...] = y
        else:
            for h in range(GDN_HEADS):
                yh = y[:, :, h * LANE:(h + 1) * LANE]
                yn = yh * lax.rsqrt(jnp.sum(yh * yh, axis=-1, keepdims=True) + NORM_EPS)
                o_ref[:, :, h * LANE:(h + 1) * LANE] = yn * mult if mult != 1.0 else yn
    x = gin_ref[...]
    lane = lax.broadcasted_iota(jnp.int32, x.shape, 2)
    g = -jnp.exp(alog_ref[...]) * _softplus(x + dtb_ref[...])
    bg_ref[...] = jnp.where(lane < GDN_HEADS, _sigmoid(x), jnp.where(lane < 2 * GDN_HEADS, g, 0.0))


def gdn_pre(u3, conv0, cw, alog, dtb, bb, tt):
    nb, t, _ = u3.shape
    w = GDN_HEADS * GDN_DK
    full = lambda a: pl.BlockSpec(a.shape, lambda b, i: (0,) * a.ndim)
    blk = lambda c: pl.BlockSpec((bb, tt, w), lambda b, i: (b, i, c // w))
    st = lambda k: pl.BlockSpec((bb, CONV_W - 1, w), lambda b, i: (b, 0, k))
    wsp = lambda k: pl.BlockSpec((CONV_W, w), lambda b, i: (0, k))
    oblk = pl.BlockSpec((bb, tt, w), lambda b, i: (b, i, 0))
    ost = pl.BlockSpec((bb, CONV_W - 1, w), lambda b, i: (b, 0, 0))
    return pl.pallas_call(
        functools.partial(_gdn_pre_kernel, tt=tt),
        grid=(nb // bb, t // tt),
        in_specs=[blk(C_GQ), blk(C_GQ + w), blk(C_GQ + 2 * w),
                  pl.BlockSpec((bb, tt, LANE), lambda b, i: (b, i, C_GB // LANE)),
                  st(0), st(1), st(2), wsp(0), wsp(1), wsp(2), full(alog), full(dtb)],
        out_specs=[oblk, oblk, oblk, pl.BlockSpec((bb, tt, LANE), lambda b, i: (b, i, 0)), ost, ost, ost],
        out_shape=[jax.ShapeDtypeStruct((nb, t, w), F32)] * 3 + [jax.ShapeDtypeStruct((nb, t, LANE), F32)]
        + [jax.ShapeDtypeStruct((nb, CONV_W - 1, w), F32)] * 3,
        scratch_shapes=[pltpu.VMEM((bb, SUBLANE + tt, w), F32)] * 3,
        compiler_params=_cp(("parallel", "arbitrary")),
        name="gdn_pre",
    )(u3, u3, u3, u3, conv0, conv0, conv0, cw, cw, cw, alog, dtb)


def _gdn_chunk_kernel(q_ref, k_ref, v_ref, bg_ref, z_ref, s0_ref, ng_ref, o_ref, s_ref, *, c):
    @pl.when(pl.program_id(1) == 0)
    def _():
        s_ref[...] = s0_ref[...]

    ri = lax.broadcasted_iota(jnp.int32, (c, c), 0)
    ci = lax.broadcasted_iota(jnp.int32, (c, c), 1)
    causal, strict = ri >= ci, ri > ci
    eye = (ri == ci).astype(F32)
    bg = bg_ref[...]
    gcum = _dot_hi(causal.astype(F32), bg)
    for h in range(GDN_HEADS):
        sl = slice(h * LANE, (h + 1) * LANE)
        q, k, v = q_ref[:, sl], k_ref[:, sl], v_ref[:, sl]
        beta = bg[:, h:h + 1]
        gc = gcum[:, GDN_HEADS + h:GDN_HEADS + h + 1]
        gc_row = jnp.sum(eye * gc, axis=0, keepdims=True)
        decay = jnp.where(causal, jnp.exp(jnp.where(causal, gc - gc_row, 0.0)), 0.0)
        kb = k * beta
        lower = jnp.where(strict, _dot_nt(kb, k) * decay, 0.0)
        inv = eye - lower
        pw = _dot_hi(lower, lower)
        n = 2
        while n < c:
            inv = inv + _dot_hi(inv, pw)
            n *= 2
            if n < c:
                pw = _dot_hi(pw, pw)
        uu = _dot_hi(inv, v * beta)
        ww = _dot_hi(inv, kb * jnp.exp(gc))
        qk = jnp.where(causal, _dot_nt(q, k) * decay, 0.0)
        s = s_ref[h]
        v_new = uu - _dot(ww, s)
        o = _dot(q * jnp.exp(gc), s) + _dot(qk, v_new)
        g_last = gc[c - 1:c, :]
        kd = k * jnp.exp(g_last - gc)
        s_ref[h] = s * jnp.exp(g_last) + lax.dot_general(kd.astype(BF16), v_new.astype(BF16), (((0,), (0,)), ((), ())),
                                                         preferred_element_type=F32)
        zz = z_ref[:, sl]
        o_ref[:, sl] = _rms(o, ng_ref[...]) * (zz * _sigmoid(zz))


def gdn_chunk(q, k, v, bg, u3, s0, ng, c):
    nb, t, w = q.shape
    blk = pl.BlockSpec((None, c, w), lambda b, n: (b, n, 0))
    sblk = pl.BlockSpec((None, GDN_HEADS, GDN_DK, GDN_DV), lambda b, n: (b, 0, 0, 0))
    return pl.pallas_call(
        functools.partial(_gdn_chunk_kernel, c=c),
        grid=(nb, t // c),
        in_specs=[blk, blk, blk, pl.BlockSpec((None, c, LANE), lambda b, n: (b, n, 0)),
                  pl.BlockSpec((None, c, w), lambda b, n: (b, n, C_GZ // w)), sblk,
                  pl.BlockSpec(ng.shape, lambda b, n: (0, 0))],
        out_specs=[blk, sblk],
        out_shape=[jax.ShapeDtypeStruct((nb, t, w), F32), jax.ShapeDtypeStruct(s0.shape, F32)],
        compiler_params=_cp(("parallel", "arbitrary")),
        name="gdn_chunk",
    )(q, k, v, bg, u3, s0, ng)


def _paged_attn_kernel(pt_ref, *refs, kind, layer, pp, nc, npages, tnew, scale):
    nheads = MLA_HEADS
    rows = nheads * tnew
    if kind == "mla":
        (q_ref, kn_ref, rn_ref, wuv_ref, ck_hbm, cr_hbm, o_ref,
         kbuf, rbuf, sem, m_ref, l_ref, acc_ref) = refs
        streams = ((ck_hbm, kbuf), (cr_hbm, rbuf))
    else:
        (q_ref, kn_ref, vn_ref, lfn_ref, ck_hbm, cv_hbm, clf_hbm, o_ref,
         kbuf, vbuf, sem, lfbuf, lfsem, cum_ref, tot_ref, cq_ref, m_ref, l_ref, acc_ref) = refs
        streams = ((ck_hbm, kbuf), (cv_hbm, vbuf))
    b, c = pl.program_id(0), pl.program_id(1)
    nb = pl.num_programs(0)
    step = b * nc + c
    slot = lax.rem(step, 2)

    def page_copies(bb, cc, sl):
        out = []
        for p in range(pp):
            page = pt_ref[bb * npages + cc * pp + p]
            for si, (hbm, buf) in enumerate(streams):
                out.append(pltpu.make_async_copy(hbm.at[layer, page], buf.at[sl, p], sem.at[sl, si]))
        return out

    def logf_copies(bb, sl):
        return [pltpu.make_async_copy(clf_hbm.at[layer, pl.ds(pt_ref[bb * npages + p], 1)],
                                      lfbuf.at[sl, pl.ds(p, 1)], lfsem.at[sl]) for p in range(npages)]

    @pl.when(step == 0)
    def _():
        for cp in page_copies(0, 0, 0):
            cp.start()
        if kind == "fox":
            for cp in logf_copies(0, 0):
                cp.start()

    @pl.when(step + 1 < nb * nc)
    def _():
        nxt = step + 1
        for cp in page_copies(nxt // nc, lax.rem(nxt, nc), 1 - slot):
            cp.start()

    if kind == "fox":
        @pl.when(jnp.logical_and(c == 0, b + 1 < nb))
        def _():
            for cp in logf_copies(b + 1, 1 - lax.rem(b, 2)):
                cp.start()

    q = q_ref[...].reshape(rows, q_ref.shape[-1])
    rowq = lax.rem(lax.broadcasted_iota(jnp.int32, (rows, LANE), 0), tnew)
    lanei = lax.broadcasted_iota(jnp.int32, (rows, LANE), 1)

    @pl.when(c == 0)
    def _():
        m_ref[...] = jnp.full(m_ref.shape, -jnp.inf, F32)
        l_ref[...] = jnp.zeros_like(l_ref)
        acc_ref[...] = jnp.zeros_like(acc_ref)
        if kind == "fox":
            bs = lax.rem(b, 2)
            for cp in logf_copies(b, bs):
                cp.wait()
            x = lfbuf[bs]
            w = PAGE * FOX_HEADS
            r = lax.broadcasted_iota(jnp.int32, (w, w), 0)
            cc = lax.broadcasted_iota(jnp.int32, (w, w), 1)
            same_head = lax.rem(r, FOX_HEADS) == cc // PAGE
            w_in = jnp.logical_and(same_head, r // FOX_HEADS <= lax.rem(cc, PAGE)).astype(F32)
            within = _dot_hi(x, w_in)
            tot = _dot_hi(x, same_head.astype(F32))
            pr = lax.broadcasted_iota(jnp.int32, (npages, npages), 0)
            pc = lax.broadcasted_iota(jnp.int32, (npages, npages), 1)
            offs = _dot_hi((pr > pc).astype(F32), tot)
            cum_ref[...] = within.reshape(cum_ref.shape)
            cum_ref[...] = cum_ref[...] + offs.reshape(cum_ref.shape)
            total = offs[npages - 1:npages, :] + tot[npages - 1:npages, :]
            lfn = lfn_ref[...]
            ncum = lfn
            sh = 1
            while sh < tnew:
                ncum = ncum + jnp.where(lax.broadcasted_iota(jnp.int32, ncum.shape, 0) >= sh, pltpu.roll(ncum, sh, 0), 0.0)
                sh *= 2
            for h in range(nheads):
                cn = total[:, h * PAGE:h * PAGE + 1] + ncum[:, h:h + 1]
                cq_ref[h * tnew:(h + 1) * tnew, :] = cn
                eye = (lax.broadcasted_iota(jnp.int32, (tnew, LANE), 0) == lax.broadcasted_iota(jnp.int32, (tnew, LANE), 1))
                tot_ref[h:h + 1, :] = jnp.sum(jnp.where(eye, cn, 0.0), axis=0, keepdims=True)

    for cp in page_copies(b, c, slot):
        cp.wait()

    def online_update(s, v):
        m_prev = m_ref[...]
        m_new = jnp.maximum(m_prev, jnp.max(s, axis=-1, keepdims=True))
        alpha = jnp.exp(m_prev - m_new)
        p = jnp.exp(s - m_new)
        l_ref[...] = alpha * l_ref[...] + jnp.sum(p, axis=-1, keepdims=True)
        acc_ref[...] = alpha * acc_ref[...] + jnp.dot(p.astype(BF16), v, preferred_element_type=F32)
        m_ref[...] = m_new

    keys = kbuf[slot].reshape(pp * PAGE, kbuf.shape[-1]).astype(BF16)
    if kind == "mla":
        ropes = rbuf[slot].reshape(pp * PAGE, MLA_ROPE).astype(BF16)
        s = (_dot_nt(q[:, 0:LANE], keys) + _dot_nt(q[:, LANE:LANE + MLA_ROPE], ropes)) * scale
        online_update(s, keys)
    else:
        vals = vbuf[slot].reshape(pp * PAGE, FOX_HD).astype(BF16)
        s = _dot_nt(q, keys) * scale
        ck = jnp.concatenate(
            [jnp.concatenate([jnp.broadcast_to(cum_ref[c, p:p + 1, h * PAGE:(h + 1) * PAGE], (tnew, PAGE))
                              for h in range(nheads)], axis=0) for p in range(pp)], axis=1)
        online_update(s + (cq_ref[...] - ck), vals)

    @pl.when(c == nc - 1)
    def _():
        zpad = jnp.zeros((LANE - tnew, LANE), F32)
        kn = jnp.concatenate([kn_ref[...], zpad], axis=0)
        if kind == "mla":
            rn = jnp.concatenate([rn_ref[...], jnp.zeros((LANE - tnew, MLA_ROPE), F32)], axis=0)
            s = (_dot_nt(q[:, 0:LANE], kn) + _dot_nt(q[:, LANE:LANE + MLA_ROPE], rn)) * scale
            vn = kn
        else:
            vn = jnp.concatenate([vn_ref[...], zpad], axis=0)
            s = _dot_nt(q, kn) * scale
            ckn = jnp.concatenate([jnp.broadcast_to(tot_ref[h:h + 1, :], (tnew, LANE)) for h in range(nheads)], axis=0)
            s = s + (cq_ref[...] - ckn)
        s = jnp.where(lanei <= rowq, s, -jnp.inf)
        online_update(s, vn.astype(BF16))
        o = acc_ref[...] / l_ref[...]
        for h in range(nheads):
            oh = o[h * tnew:(h + 1) * tnew, :]
            o_ref[:, h * LANE:(h + 1) * LANE] = _dot(oh, wuv_ref[h]) if kind == "mla" else oh


def paged_attn(kind, layer, page_table, q, new_parts, caches, wuv, scale, pp):
    nheads, m, dq = q.shape
    nbatch, npages = page_table.shape
    tnew = m // nbatch
    nc = npages // pp
    rows = nheads * tnew
    qspec = pl.BlockSpec((nheads, tnew, dq), lambda b, c, pt: (0, b, 0))
    newspec = lambda a: pl.BlockSpec((tnew, a.shape[1]), lambda b, c, pt: (b, 0))
    anyspec = pl.BlockSpec(memory_space=pl.ANY)
    in_specs = [qspec] + [newspec(a) for a in new_parts]
    args = [q] + list(new_parts)
    if kind == "mla":
        in_specs.append(pl.BlockSpec(wuv.shape, lambda b, c, pt: (0, 0, 0)))
        args.append(wuv)
        scratch = [pltpu.VMEM((2, pp, PAGE, MLA_KV_LORA), F32), pltpu.VMEM((2, pp, PAGE, MLA_ROPE), F32),
                   pltpu.SemaphoreType.DMA((2, 2))]
    else:
        scratch = [pltpu.VMEM((2, pp, PAGE, FOX_HD), F32), pltpu.VMEM((2, pp, PAGE, FOX_HD), F32),
                   pltpu.SemaphoreType.DMA((2, 2)),
                   pltpu.VMEM((2, npages, PAGE * FOX_HEADS), F32), pltpu.SemaphoreType.DMA((2,)),
                   pltpu.VMEM((nc, pp, PAGE * FOX_HEADS), F32), pltpu.VMEM((SUBLANE, LANE), F32),
                   pltpu.VMEM((rows, 1), F32)]
    in_specs += [anyspec] * len(caches)
    args += list(caches)
    scratch += [pltpu.VMEM((rows, 1), F32), pltpu.VMEM((rows, 1), F32), pltpu.VMEM((rows, LANE), F32)]
    return pl.pallas_call(
        functools.partial(_paged_attn_kernel, kind=kind, layer=layer, pp=pp, nc=nc, npages=npages, tnew=tnew, scale=scale),
        grid_spec=pltpu.PrefetchScalarGridSpec(
            num_scalar_prefetch=1, grid=(nbatch, nc), in_specs=in_specs,
            out_specs=pl.BlockSpec((tnew, nheads * LANE), lambda b, c, pt: (b, 0)),
            scratch_shapes=scratch),
        out_shape=jax.ShapeDtypeStruct((m, nheads * LANE), F32),
        compiler_params=_cp(("arbitrary", "arbitrary")),
        name="paged_attn_" + kind,
    )(page_table.reshape(-1), *args)


def _pad_lanes(a, width, offset=0):
    a = a.reshape(1, -1)
    return jnp.pad(a, ((0, 0), (offset, width - offset - a.shape[1])))


def _layer_params(l, w_in, mla_q_a_g, mla_w_uq, mla_q_norm_g, mla_kv_norm_g, mla_kpe_norm_g, mla_w_uk, mla_w_uv,
                  fox_q_norm_g, fox_k_norm_g, fox_f_bias, lru_conv_w, lru_conv_b, lru_w_a, lru_b_a, lru_w_x, lru_b_x,
                  lru_lambda, gdn_conv_w, gdn_a_log, gdn_dt_bias, gdn_norm_g, w_out, w_gate, w_up, w_down):
    w = w_in[l]
    d = w.shape[0]
    z = lambda n: jnp.zeros((d, n), w.dtype)
    n_qkv = MLA_Q_LORA + MLA_KV_LORA
    n_mla = n_qkv + MLA_ROPE
    n_fq = FOX_HEADS * FOX_HD
    n_fox = n_fq + 2 * FOX_HD + FOX_HEADS
    n_main = w.shape[1] - 2 * GDN_HEADS
    w_in_p = jnp.concatenate([w[:, :n_qkv], w[:, n_mla:n_mla + n_fq], w[:, n_qkv:n_mla], z(LANE - MLA_ROPE),
                              w[:, n_mla + n_fq:n_mla + n_fox], z(LANE - FOX_HEADS),
                              w[:, n_mla + n_fox:n_main], w[:, n_main:], z(D_INP - C_GB - 2 * GDN_HEADS)], axis=1).astype(BF16)
    uq = mla_w_uq[l]
    wuq = jnp.concatenate([uq[:, :, :MLA_NOPE].reshape(MLA_Q_LORA, -1),
                           jnp.pad(uq[:, :, MLA_NOPE:], ((0, 0), (0, 0), (0, LANE - MLA_ROPE))).reshape(MLA_Q_LORA, -1)],
                          axis=1).astype(BF16)
    return dict(
        w_in=w_in_p,
        gqa=mla_q_a_g[l].reshape(1, -1), wuq=wuq,
        gqn=mla_q_norm_g[l, :MLA_NOPE].reshape(1, -1), gqr=_pad_lanes(mla_q_norm_g[l, MLA_NOPE:], LANE),
        gkv=mla_kv_norm_g[l].reshape(1, -1), gkpe=_pad_lanes(mla_kpe_norm_g[l], LANE),
        wuk=jnp.transpose(mla_w_uk[l], (1, 2, 0)).astype(BF16), wuv=jnp.transpose(mla_w_uv[l], (1, 0, 2)).astype(BF16),
        fgq=fox_q_norm_g[l].reshape(1, -1), fgk=fox_k_norm_g[l].reshape(1, -1), fb=_pad_lanes(fox_f_bias[l], LANE),
        lcw=lru_conv_w[l], lcb=lru_conv_b[l].reshape(1, -1), lwa=lru_w_a[l].astype(BF16), lba=lru_b_a[l].reshape(1, -1),
        lwx=lru_w_x[l].astype(BF16), lbx=lru_b_x[l].reshape(1, -1), lam=lru_lambda[l].reshape(1, -1),
        gcw=gdn_conv_w[l], alog=_pad_lanes(gdn_a_log[l], LANE, GDN_HEADS), dtb=_pad_lanes(gdn_dt_bias[l], LANE, GDN_HEADS),
        gng=gdn_norm_g[l].reshape(1, -1),
        w_out=w_out[l].astype(BF16), w_gate=w_gate[l].astype(BF16), w_up=w_up[l].astype(BF16), w_down=w_down[l].astype(BF16),
    )


def _tiles(t):
    if t >= 512:
        return dict(tm=512, tm_mm=1024, bb=1, tt=512, tq=256, tk=512)
    return dict(tm=512, tm_mm=512, bb=512 // t, tt=t, tq=None, tk=None)


def _trunk_layer(l, p, x, mods, nb, t, g_mix, g_ffn, tabs, past):
    sh1, sc1, g1, sh2, sc2, g2 = mods
    m = nb * t
    ti = _tiles(t)
    tm = ti["tm"]
    u = norm_mod_matmul(x, g_mix, sc1, sh1, p["w_in"], t, ti["tm_mm"], 1024)
    u3 = u.reshape(nb, t, D_INP)
    q, lat, rot, kfull = mla_pre(u, tabs, tabs[0].shape[0] // tm, p["gqa"], p["wuq"], p["gqn"], p["gqr"], p["gkv"],
                                 p["gkpe"], p["wuk"], tm)
    mla_scale = (MLA_NOPE + MLA_ROPE) ** -0.5
    fox_scale = FOX_HD ** -0.5
    if past is None:
        o_lat = causal_attn(q, kfull, None, None, None, nb, mla_scale, MLA_KV_LORA, ti["tq"], ti["tk"], False)
        o_mla = head_proj(o_lat, p["wuv"], tm)
        fq, fk, fv, lf, ccol, crow = fox_pre(u, p["fgq"], p["fgk"], p["fb"], nb, tm, True)
        o_fox = causal_attn(fq, fk, fv, ccol, crow, nb, fox_scale, FOX_HD, ti["tq"], ti["tk"], True)
        h0 = jnp.zeros((nb, 1, LRU_WIDTH), F32)
        lconv0 = jnp.zeros((nb, CONV_W - 1, LRU_WIDTH), F32)
        s0 = jnp.zeros((nb, GDN_HEADS, GDN_DK, GDN_DV), F32)
        gconv0 = jnp.zeros((nb, CONV_W - 1, 3 * GDN_HEADS * GDN_DK), F32)
    else:
        page_table, c_lat, c_rope, c_k, c_v, c_lf, h0, lconv0, s0, gconv0 = past
        pp = min(PAGES_PER_STEP, page_table.shape[1])
        o_mla = paged_attn("mla", l, page_table, q, [lat, rot], [c_lat, c_rope], p["wuv"], mla_scale, pp)
        fq, fk, fv, lf = fox_pre(u, p["fgq"], p["fgk"], p["fb"], 1, tm, False)
        lf_pad = jnp.pad(lf, ((0, 0), (0, LANE - FOX_HEADS)))
        o_fox = paged_attn("fox", l, page_table, fq, [fk, fv, lf_pad], [c_k, c_v, c_lf], None, fox_scale, pp)
        h0 = h0[:, None, :]
    y_lru, h_new, lconv_new = lru_mixer(u3, h0, lconv0, p["lcw"], p["lcb"], p["lwa"], p["lba"], p["lwx"], p["lbx"],
                                        p["lam"], ti["bb"], ti["tt"])
    gq, gk, gv, bg, nq, nk, nv = gdn_pre(u3, gconv0, p["gcw"], p["alog"], p["dtb"], ti["bb"], ti["tt"])
    o_gdn, s_new = gdn_chunk(gq, gk, gv, bg, u3, s0, p["gng"], math.gcd(t, GDN_CHUNK))
    w = MLA_HEADS * LANE
    x = matmul_residual([o_mla, o_fox, y_lru.reshape(m, w), o_gdn.reshape(m, w)], p["w_out"], x, g1, t, ti["tm_mm"], 1024)
    a = ffn_up(x, g_ffn, sc2, sh2, p["w_gate"], p["w_up"], t, ti["tm_mm"], 512)
    x = matmul_residual([a], p["w_down"], x, g2, t, 512, 512)
    states = (lat.reshape(nb, t, -1), rot.reshape(nb, t, -1), fk.reshape(nb, t, 1, FOX_HD), fv.reshape(nb, t, 1, FOX_HD),
              lf.reshape(nb, t, FOX_HEADS), h_new.reshape(nb, LRU_WIDTH), lconv_new, s_new,
              jnp.concatenate([nq, nk, nv], axis=-1))
    return x, states


def kernel(x_prompt, x_sample, cache_mla_lat, cache_mla_rope, cache_fox_k, cache_fox_v, cache_fox_logf, state_lru_h, state_lru_conv, state_gdn_S, state_gdn_conv, page_table, c_prompt, c_sample, w_ada, b_ada, norm_mix_g, w_in, mla_q_a_g, mla_w_uq, mla_q_norm_g, mla_kv_norm_g, mla_kpe_norm_g, mla_w_uk, mla_w_uv, fox_q_norm_g, fox_k_norm_g, fox_f_bias, lru_conv_w, lru_conv_b, lru_w_a, lru_b_a, lru_w_x, lru_b_x, lru_lambda, gdn_conv_w, gdn_a_log, gdn_dt_bias, gdn_norm_g, w_out, norm_ffn_g, w_gate, w_up, w_down):
    bp, tp, d = x_prompt.shape
    bs, ts, _ = x_sample.shape
    n_pool = cache_mla_lat.shape[1]
    n_past = page_table.shape[1] * PAGE
    c_k = cache_fox_k.reshape(DEPTH, n_pool, PAGE, FOX_HD)
    c_v = cache_fox_v.reshape(DEPTH, n_pool, PAGE, FOX_HD)
    c_lf = cache_fox_logf.reshape(DEPTH, n_pool, PAGE * FOX_HEADS)
    tabs_p = rope_tables(jnp.arange(tp, dtype=jnp.int32))
    tm_s = _tiles(ts)["tm"]
    tabs_s = rope_tables(n_past + jnp.arange(tm_s, dtype=jnp.int32) % ts)
    c_all = jnp.concatenate([c_sample, c_prompt], axis=0)
    x_p = x_prompt.reshape(bp * tp, d)
    x_s = x_sample.reshape(bs * ts, d)
    states_p, states_s = [], []
    for l in range(DEPTH):
        p = _layer_params(l, w_in, mla_q_a_g, mla_w_uq, mla_q_norm_g, mla_kv_norm_g, mla_kpe_norm_g, mla_w_uk, mla_w_uv,
                          fox_q_norm_g, fox_k_norm_g, fox_f_bias, lru_conv_w, lru_conv_b, lru_w_a, lru_b_a, lru_w_x,
                          lru_b_x, lru_lambda, gdn_conv_w, gdn_a_log, gdn_dt_bias, gdn_norm_g, w_out, w_gate, w_up, w_down)
        mod = ada_mod(c_all, w_ada, b_ada, l)
        mods = jnp.split(mod, 6, axis=-1)
        mods_p = [mm[bs:, None, :] for mm in mods]
        mods_s = [jnp.repeat(mm[:bs], ts, axis=0) for mm in mods]
        x_p, st_p = _trunk_layer(l, p, x_p, mods_p, bp, tp, norm_mix_g[l], norm_ffn_g[l], tabs_p, None)
        past = (page_table, cache_mla_lat, cache_mla_rope, c_k, c_v, c_lf, state_lru_h[l], state_lru_conv[l],
                state_gdn_S[l], state_gdn_conv[l])
        x_s, st_s = _trunk_layer(l, p, x_s, mods_s, bs, ts, norm_mix_g[l], norm_ffn_g[l], tabs_s, past)
        states_p.append(st_p)
        states_s.append(st_s)
    sp = [jnp.stack(a, axis=0) for a in zip(*states_p)]
    ss = [jnp.stack(a, axis=0) for a in zip(*states_s)]
    out = [x_p.reshape(bp, tp, d), x_s.reshape(bs, ts, d)]
    for a, b in zip(sp, ss):
        out += [a, b]
    return tuple(out)
```

```python
import functools
import math

import jax
import jax.numpy as jnp
from jax import lax
from jax.experimental import pallas as pl
from jax.experimental.pallas import tpu as pltpu

F32 = jnp.float32
BF16 = jnp.bfloat16
HI = lax.Precision.HIGHEST

D_MODEL = 2048
DEPTH = 2
PAGE = 128
MLA_HEADS = 4
MLA_Q_LORA = 384
MLA_KV_LORA = 128
MLA_NOPE = 128
MLA_ROPE = 64
ROPE_THETA = 10000.0
FOX_HEADS = 4
FOX_HD = 128
LRU_WIDTH = 512
LRU_BLOCKS = 4
LRU_C = 8.0
GDN_HEADS = 4
GDN_DK = 128
GDN_DV = 128
GDN_CHUNK = 64
CONV_W = 4
NORM_EPS = 1e-6
LANE = 128
SUBLANE = 8
VMEM_LIMIT = 56 * 1024 * 1024
PAGES_PER_STEP = 64

C_QA, C_KVA, C_KPE = 0, 384, 1024
C_FQ, C_FK, C_FV, C_FL = 512, 1152, 1280, 1408
C_LX, C_LG = 1536, 2048
C_GQ, C_GZ, C_GB = 2560, 4096, 4608
D_INP = 5120


def _cp(sem):
    return pltpu.CompilerParams(dimension_semantics=sem, vmem_limit_bytes=VMEM_LIMIT)


def _sigmoid(x):
    return 1.0 / (1.0 + jnp.exp(-x))


def _softplus(x):
    return jnp.maximum(x, 0.0) + jnp.log1p(jnp.exp(-jnp.abs(x)))


def _dot(a, b):
    return jnp.dot(a.astype(BF16), b.astype(BF16), preferred_element_type=F32)


def _dot_nt(a, b):
    return lax.dot_general(a.astype(BF16), b.astype(BF16), (((1,), (1,)), ((), ())), preferred_element_type=F32)


def _dot_hi(a, b):
    return jnp.dot(a, b, precision=HI, preferred_element_type=F32)


def _split(a):
    hi = a.astype(BF16)
    return hi, (a - hi.astype(F32)).astype(BF16)


def _dot3(a, b):
    (ah, al), (bh, bl) = a, b
    d = lambda x, y: jnp.dot(x, y, preferred_element_type=F32)
    return d(ah, bh) + (d(ah, bl) + d(al, bh))


def _rms(x, g, n=None):
    n = x.shape[-1] if n is None else n
    ms = jnp.sum(x * x, axis=-1, keepdims=True) * (1.0 / n)
    return x * lax.rsqrt(ms + NORM_EPS) * g


def _ada_kernel(c_ref, w_ref, b_ref, o_ref):
    c = c_ref[...]
    o_ref[...] = _dot(c * _sigmoid(c), w_ref[...]) + b_ref[...]


def ada_mod(c_all, w_ada, b_ada, layer):
    m, d = c_all.shape
    n = w_ada.shape[-1]
    tn = 1024
    return pl.pallas_call(
        _ada_kernel,
        grid=(n // tn,),
        in_specs=[pl.BlockSpec((m, d), lambda j: (0, 0)),
                  pl.BlockSpec((None, d, tn), lambda j: (layer, 0, j)),
                  pl.BlockSpec((None, 1, tn), lambda j: (layer, 0, j))],
        out_specs=pl.BlockSpec((m, tn), lambda j: (0, j)),
        out_shape=jax.ShapeDtypeStruct((m, n), F32),
        compiler_params=_cp(("arbitrary",)),
        name="ada_mod",
    )(c_all, w_ada, b_ada.reshape(DEPTH, 1, n))


def _mod_spec(mod, tm, rows_per_group):
    if mod.ndim == 3:
        bpg = rows_per_group // tm
        return pl.BlockSpec((None, 1, mod.shape[-1]), lambda i, j: (i // bpg, 0, 0))
    return pl.BlockSpec((tm, mod.shape[-1]), lambda i, j: (i, 0))


def _norm_mod(x_ref, g_ref, sc_ref, sh_ref, h_ref):
    x = x_ref[...]
    y = x * lax.rsqrt(jnp.mean(x * x, axis=-1, keepdims=True) + NORM_EPS) * g_ref[...]
    h_ref[...] = (y * (1.0 + sc_ref[...]) + sh_ref[...]).astype(BF16)


def _nmm_kernel(x_ref, g_ref, sc_ref, sh_ref, w_ref, o_ref, h_ref):
    @pl.when(pl.program_id(1) == 0)
    def _():
        _norm_mod(x_ref, g_ref, sc_ref, sh_ref, h_ref)
    o_ref[...] = jnp.dot(h_ref[...], w_ref[...], preferred_element_type=F32)


def norm_mod_matmul(x, g, scale, shift, w, rows_per_group, tm, tn):
    m, d = x.shape
    n = w.shape[1]
    return pl.pallas_call(
        _nmm_kernel,
        grid=(m // tm, n // tn),
        in_specs=[pl.BlockSpec((tm, d), lambda i, j: (i, 0)),
                  pl.BlockSpec((1, d), lambda i, j: (0, 0)),
                  _mod_spec(scale, tm, rows_per_group),
                  _mod_spec(shift, tm, rows_per_group),
                  pl.BlockSpec((d, tn), lambda i, j: (0, j))],
        out_specs=pl.BlockSpec((tm, tn), lambda i, j: (i, j)),
        out_shape=jax.ShapeDtypeStruct((m, n), F32),
        scratch_shapes=[pltpu.VMEM((tm, d), BF16)],
        compiler_params=_cp(("parallel", "arbitrary")),
        name="norm_mod_matmul",
    )(x, g.reshape(1, d), scale, shift, w)


def _ffn_up_kernel(x_ref, g_ref, sc_ref, sh_ref, wg_ref, wu_ref, o_ref, h_ref):
    @pl.when(pl.program_id(1) == 0)
    def _():
        _norm_mod(x_ref, g_ref, sc_ref, sh_ref, h_ref)
    h = h_ref[...]
    a = jnp.dot(h, wg_ref[...].astype(BF16), preferred_element_type=F32)
    b = jnp.dot(h, wu_ref[...].astype(BF16), preferred_element_type=F32)
    o_ref[...] = (a * _sigmoid(a) * b).astype(BF16)


def ffn_up(x, g, scale, shift, wg, wu, layer, rows_per_group, tm, tn):
    m, d = x.shape
    n = wg.shape[2]
    return pl.pallas_call(
        _ffn_up_kernel,
        grid=(m // tm, n // tn),
        in_specs=[pl.BlockSpec((tm, d), lambda i, j: (i, 0)),
                  pl.BlockSpec((1, d), lambda i, j: (0, 0)),
                  _mod_spec(scale, tm, rows_per_group),
                  _mod_spec(shift, tm, rows_per_group),
                  pl.BlockSpec((None, d, tn), lambda i, j: (layer, 0, j)),
                  pl.BlockSpec((None, d, tn), lambda i, j: (layer, 0, j))],
        out_specs=pl.BlockSpec((tm, tn), lambda i, j: (i, j)),
        out_shape=jax.ShapeDtypeStruct((m, n), BF16),
        scratch_shapes=[pltpu.VMEM((tm, d), BF16)],
        compiler_params=_cp(("parallel", "arbitrary")),
        name="ffn_up",
    )(x, g.reshape(1, d), scale, shift, wg, wu)


def _mm_res_kernel(*refs, n_in):
    a_refs, w_refs = refs[:n_in], refs[n_in:2 * n_in]
    x_ref, gate_ref, o_ref = refs[2 * n_in:]
    acc = _dot(a_refs[0][...], w_refs[0][...])
    for a_ref, w_ref in zip(a_refs[1:], w_refs[1:]):
        acc = acc + _dot(a_ref[...], w_ref[...])
    o_ref[...] = x_ref[...] + gate_ref[...] * acc


def matmul_residual(a_list, w, layer, x, gate, rows_per_group, tm, tn):
    m, d = x.shape
    n_in = len(a_list)
    kk = a_list[0].shape[1]
    in_specs = [pl.BlockSpec((tm, kk), lambda i, j: (i, 0)) for _ in a_list]
    in_specs += [pl.BlockSpec((None, kk, tn), lambda i, j, k=k: (layer, k, j)) for k in range(n_in)]
    in_specs += [pl.BlockSpec((tm, tn), lambda i, j: (i, j))]
    if gate.ndim == 3:
        bpg = rows_per_group // tm
        in_specs += [pl.BlockSpec((None, 1, tn), lambda i, j: (i // bpg, 0, j))]
    else:
        in_specs += [pl.BlockSpec((tm, tn), lambda i, j: (i, j))]
    return pl.pallas_call(
        functools.partial(_mm_res_kernel, n_in=n_in),
        grid=(m // tm, d // tn),
        in_specs=in_specs,
        out_specs=pl.BlockSpec((tm, tn), lambda i, j: (i, j)),
        out_shape=jax.ShapeDtypeStruct((m, d), F32),
        compiler_params=_cp(("parallel", "arbitrary")),
        name="matmul_residual",
    )(*a_list, *([w] * n_in), x, gate)


def rope_tables(pos):
    half = MLA_ROPE // 2
    freqs = ROPE_THETA ** (-jnp.arange(half, dtype=F32) / half)
    ang = pos.astype(F32)[:, None] * freqs[None, :]
    cos, sin = jnp.cos(ang), jnp.sin(ang)
    z = jnp.zeros_like(cos)
    tc = jnp.concatenate([cos, cos, z, z], axis=-1)
    ts1 = jnp.concatenate([-sin, z, z, z], axis=-1)
    ts2 = jnp.concatenate([z, sin, z, z], axis=-1)
    return tc, ts1, ts2


def _rope(x, tc, ts1, ts2):
    return x * tc + pltpu.roll(x, LANE - MLA_ROPE // 2, 1) * ts1 + pltpu.roll(x, MLA_ROPE // 2, 1) * ts2


def _mla_pre_kernel(qa_ref, kva_ref, kpe_ref, tc_ref, ts1_ref, ts2_ref, gqa_ref, wuq_ref, gqn_ref, gqr_ref,
                    gkv_ref, gkpe_ref, wuk_ref, q_ref, lat_ref, rot_ref, kf_ref):
    tc, ts1, ts2 = tc_ref[...], ts1_ref[...], ts2_ref[...]
    qn = _rms(qa_ref[...], gqa_ref[...])
    q = _dot(qn, wuq_ref[...])
    hw = MLA_HEADS * MLA_NOPE
    for h in range(MLA_HEADS):
        nope = q[:, h * LANE:(h + 1) * LANE]
        rope = q[:, hw + h * LANE:hw + (h + 1) * LANE]
        ms = (jnp.sum(nope * nope, axis=-1, keepdims=True) + jnp.sum(rope * rope, axis=-1, keepdims=True)) * (
            1.0 / (MLA_NOPE + MLA_ROPE))
        r = lax.rsqrt(ms + NORM_EPS)
        q_ref[h, :, 0:LANE] = _dot(nope * r * gqn_ref[...], wuk_ref[h])
        q_ref[h, :, LANE:2 * LANE] = _rope(rope * r * gqr_ref[...], tc, ts1, ts2)
    lat = _rms(kva_ref[...], gkv_ref[...])
    rot = _rope(_rms(kpe_ref[...], gkpe_ref[...], MLA_ROPE), tc, ts1, ts2)
    lat_ref[...] = lat
    rot_ref[...] = rot[:, :MLA_ROPE]
    kf_ref[:, 0:LANE] = lat.astype(BF16)
    kf_ref[:, LANE:2 * LANE] = rot.astype(BF16)


def mla_pre(u, tabs, tab_blocks, gqa, wuq, gqn, gqr, gkv, gkpe, wuk, tm):
    m = u.shape[0]
    col = lambda c, w: pl.BlockSpec((tm, w), lambda i: (i, c // w))
    tab = pl.BlockSpec((tm, LANE), lambda i: (i % tab_blocks, 0))
    full = lambda a: pl.BlockSpec(a.shape, lambda i: (0,) * a.ndim)
    return pl.pallas_call(
        _mla_pre_kernel,
        grid=(m // tm,),
        in_specs=[col(C_QA, MLA_Q_LORA), col(C_KVA, LANE), col(C_KPE, LANE), tab, tab, tab,
                  full(gqa), full(wuq), full(gqn), full(gqr), full(gkv), full(gkpe), full(wuk)],
        out_specs=[pl.BlockSpec((MLA_HEADS, tm, 2 * LANE), lambda i: (0, i, 0)),
                   pl.BlockSpec((tm, LANE), lambda i: (i, 0)),
                   pl.BlockSpec((tm, MLA_ROPE), lambda i: (i, 0)),
                   pl.BlockSpec((tm, 2 * LANE), lambda i: (i, 0))],
        out_shape=[jax.ShapeDtypeStruct((MLA_HEADS, m, 2 * LANE), F32),
                   jax.ShapeDtypeStruct((m, LANE), F32),
                   jax.ShapeDtypeStruct((m, MLA_ROPE), F32),
                   jax.ShapeDtypeStruct((m, 2 * LANE), BF16)],
        compiler_params=_cp(("parallel",)),
        name="mla_pre",
    )(u, u, u, *tabs, gqa, wuq, gqn, gqr, gkv, gkpe, wuk)


def _head_proj_kernel(o_ref, w_ref, y_ref):
    for h in range(MLA_HEADS):
        y_ref[:, h * LANE:(h + 1) * LANE] = _dot(o_ref[h], w_ref[h])


def head_proj(o, w, tm):
    m = o.shape[1]
    return pl.pallas_call(
        _head_proj_kernel,
        grid=(m // tm,),
        in_specs=[pl.BlockSpec((MLA_HEADS, tm, LANE), lambda i: (0, i, 0)),
                  pl.BlockSpec(w.shape, lambda i: (0, 0, 0))],
        out_specs=pl.BlockSpec((tm, MLA_HEADS * LANE), lambda i: (i, 0)),
        out_shape=jax.ShapeDtypeStruct((m, MLA_HEADS * LANE), F32),
        compiler_params=_cp(("parallel",)),
        name="head_proj",
    )(o, w)


def _fox_pre_kernel(q_ref, k_ref, v_ref, f_ref, gq_ref, gk_ref, fb_ref, qo_ref, ko_ref, vo_ref, lf_ref, *rest, with_cum):
    for h in range(FOX_HEADS):
        qo_ref[h] = _rms(q_ref[:, h * LANE:(h + 1) * LANE], gq_ref[...])
    ko_ref[...] = _rms(k_ref[...], gk_ref[...])
    vo_ref[...] = v_ref[...]
    z = f_ref[...] + fb_ref[...]
    logf = jnp.minimum(z, 0.0) - jnp.log1p(jnp.exp(-jnp.abs(z)))
    lane = lax.broadcasted_iota(jnp.int32, logf.shape, 1)
    logf = jnp.where(lane < FOX_HEADS, logf, 0.0)
    lf_ref[...] = logf[:, :FOX_HEADS]
    if with_cum:
        cc_ref, cr_ref, carry_ref = rest
        tm = logf.shape[0]

        @pl.when(pl.program_id(1) == 0)
        def _():
            carry_ref[...] = jnp.zeros_like(carry_ref)
        tri = (lax.broadcasted_iota(jnp.int32, (tm, tm), 0) >= lax.broadcasted_iota(jnp.int32, (tm, tm), 1)).astype(F32)
        cum = _dot_hi(tri, logf) + carry_ref[...]
        carry_ref[...] = cum[tm - 1:tm, :]
        cc_ref[...] = cum[:, :FOX_HEADS]
        cr_ref[...] = cum.T[:FOX_HEADS, :]


def fox_pre(u, gq, gk, fb, nb, tm, with_cum):
    m = u.shape[0]
    t = m // nb
    nt = t // tm
    col = lambda c, w: pl.BlockSpec((tm, w), lambda b, i: (b * nt + i, c // w))
    full = lambda a: pl.BlockSpec(a.shape, lambda b, i: (0,) * a.ndim)
    row = pl.BlockSpec((tm, LANE), lambda b, i: (b * nt + i, 0))
    out_specs = [pl.BlockSpec((FOX_HEADS, tm, LANE), lambda b, i: (0, b * nt + i, 0)), row, row,
                 pl.BlockSpec((tm, FOX_HEADS), lambda b, i: (b * nt + i, 0))]
    out_shape = [jax.ShapeDtypeStruct((FOX_HEADS, m, LANE), F32), jax.ShapeDtypeStruct((m, LANE), F32),
                 jax.ShapeDtypeStruct((m, LANE), F32), jax.ShapeDtypeStruct((m, FOX_HEADS), F32)]
    scratch = []
    if with_cum:
        out_specs += [pl.BlockSpec((tm, FOX_HEADS), lambda b, i: (b * nt + i, 0)),
                      pl.BlockSpec((None, FOX_HEADS, tm), lambda b, i: (b, 0, i))]
        out_shape += [jax.ShapeDtypeStruct((m, FOX_HEADS), F32), jax.ShapeDtypeStruct((nb, FOX_HEADS, t), F32)]
        scratch = [pltpu.VMEM((1, LANE), F32)]
    return pl.pallas_call(
        functools.partial(_fox_pre_kernel, with_cum=with_cum),
        grid=(nb, nt),
        in_specs=[col(C_FQ, FOX_HEADS * FOX_HD), col(C_FK, LANE), col(C_FV, LANE), col(C_FL, LANE),
                  full(gq), full(gk), full(fb)],
        out_specs=out_specs,
        out_shape=out_shape,
        scratch_shapes=scratch,
        compiler_params=_cp(("parallel", "arbitrary")),
        name="fox_pre",
    )(u, u, u, u, gq, gk, fb)


def _causal_attn_kernel(*refs, scale, tq, tk, nheads, dv, has_bias, v_from_k, concat_heads):
    it = iter(refs)
    q_ref, k_ref = next(it), next(it)
    v_ref = k_ref if v_from_k else next(it)
    cc_ref, cr_ref = (next(it), next(it)) if has_bias else (None, None)
    o_ref, qs_ref, m_ref, l_ref, acc_ref = next(it), next(it), next(it), next(it), next(it)
    qi, ki = pl.program_id(1), pl.program_id(2)

    @pl.when(ki == 0)
    def _():
        m_ref[...] = jnp.full(m_ref.shape, -jnp.inf, F32)
        l_ref[...] = jnp.zeros_like(l_ref)
        acc_ref[...] = jnp.zeros_like(acc_ref)
        qs_ref[...] = q_ref[...].astype(BF16)

    @pl.when(ki * tk <= qi * tq + tq - 1)
    def _():
        k = k_ref[...].astype(BF16)
        v = v_ref[:, 0:dv].astype(BF16)
        row = qi * tq + lax.broadcasted_iota(jnp.int32, (tq, tk), 0)
        colk = ki * tk + lax.broadcasted_iota(jnp.int32, (tq, tk), 1)
        visible = colk <= row
        for h in range(nheads):
            s = _dot_nt(qs_ref[h], k) * scale
            if has_bias:
                s = s + (cc_ref[:, h:h + 1] - cr_ref[h:h + 1, :])
            s = jnp.where(visible, s, -jnp.inf)
            m_prev = m_ref[h]
            m_new = jnp.maximum(m_prev, jnp.max(s, axis=-1, keepdims=True))
            alpha = jnp.exp(m_prev - m_new)
            p = jnp.exp(s - m_new)
            l_ref[h] = alpha * l_ref[h] + jnp.sum(p, axis=-1, keepdims=True)
            acc_ref[h] = alpha * acc_ref[h] + jnp.dot(p.astype(BF16), v, preferred_element_type=F32)
            m_ref[h] = m_new

    @pl.when(ki == pl.num_programs(2) - 1)
    def _():
        for h in range(nheads):
            o = acc_ref[h] / l_ref[h]
            if concat_heads:
                o_ref[:, h * dv:(h + 1) * dv] = o
            else:
                o_ref[h] = o


def causal_attn(q, k, v, cum_col, cum_row, nb, scale, dv, tq, tk, concat_heads):
    nheads, m, dq = q.shape
    t = m // nb
    nq, nk = t // tq, t // tk
    kclamp = lambda qi, ki: jnp.minimum(ki, (qi * tq + tq - 1) // tk)
    in_specs = [pl.BlockSpec((nheads, tq, dq), lambda b, qi, ki: (0, b * nq + qi, 0)),
                pl.BlockSpec((tk, k.shape[1]), lambda b, qi, ki: (b * nk + kclamp(qi, ki), 0))]
    args = [q, k]
    if v is not None:
        in_specs.append(pl.BlockSpec((tk, v.shape[1]), lambda b, qi, ki: (b * nk + kclamp(qi, ki), 0)))
        args.append(v)
    if cum_col is not None:
        in_specs += [pl.BlockSpec((tq, nheads), lambda b, qi, ki: (b * nq + qi, 0)),
                     pl.BlockSpec((None, nheads, tk), lambda b, qi, ki: (b, 0, kclamp(qi, ki)))]
        args += [cum_col, cum_row]
    if concat_heads:
        out_spec = pl.BlockSpec((tq, nheads * dv), lambda b, qi, ki: (b * nq + qi, 0))
        out_shape = jax.ShapeDtypeStruct((m, nheads * dv), F32)
    else:
        out_spec = pl.BlockSpec((nheads, tq, dv), lambda b, qi, ki: (0, b * nq + qi, 0))
        out_shape = jax.ShapeDtypeStruct((nheads, m, dv), F32)
    return pl.pallas_call(
        functools.partial(_causal_attn_kernel, scale=scale, tq=tq, tk=tk, nheads=nheads, dv=dv,
                          has_bias=cum_col is not None, v_from_k=v is None, concat_heads=concat_heads),
        grid=(nb, nq, nk),
        in_specs=in_specs,
        out_specs=out_spec,
        out_shape=out_shape,
        scratch_shapes=[pltpu.VMEM((nheads, tq, dq), BF16), pltpu.VMEM((nheads, tq, 1), F32),
                        pltpu.VMEM((nheads, tq, 1), F32), pltpu.VMEM((nheads, tq, dv), F32)],
        compiler_params=_cp(("parallel", "parallel", "arbitrary")),
        name="causal_attn",
    )(*args)


def _causal_conv(x_ref, c0_ref, w_ref, xbuf_ref, cnew_ref, tt):
    lo = SUBLANE - (CONV_W - 1)

    @pl.when(pl.program_id(1) == 0)
    def _():
        xbuf_ref[:, lo:SUBLANE, :] = c0_ref[...]
    xbuf_ref[:, SUBLANE:SUBLANE + tt, :] = x_ref[...]
    y = xbuf_ref[:, lo:lo + tt, :] * w_ref[0:1, :]
    for j in range(1, CONV_W):
        y = y + xbuf_ref[:, lo + j:lo + j + tt, :] * w_ref[j:j + 1, :]
    tail = xbuf_ref[:, lo + tt:SUBLANE + tt, :]
    cnew_ref[...] = tail
    xbuf_ref[:, lo:SUBLANE, :] = tail
    return y


def _lru_kernel(x_ref, gate_ref, c0_ref, h0_ref, cw_ref, cb_ref, wa_ref, ba_ref, wx_ref, bx_ref, lam_ref,
                y_ref, hl_ref, cnew_ref, xbuf_ref, hcar_ref, abuf_ref, bbuf_ref, *, tt):
    bb = x_ref.shape[0]

    @pl.when(pl.program_id(1) == 0)
    def _():
        hcar_ref[...] = h0_ref[...]
        abuf_ref[:, 0:tt, :] = jnp.ones((bb, tt, LRU_WIDTH), F32)
        bbuf_ref[:, 0:tt, :] = jnp.zeros((bb, tt, LRU_WIDTH), F32)

    xc = _causal_conv(x_ref, c0_ref, cw_ref, xbuf_ref, cnew_ref, tt) + cb_ref[...]
    xc2 = xc.reshape(bb * tt, LRU_WIDTH)
    bw = LRU_WIDTH // LRU_BLOCKS
    ra = jnp.concatenate([_dot(xc2[:, n * bw:(n + 1) * bw], wa_ref[n]) for n in range(LRU_BLOCKS)], axis=-1)
    rx = jnp.concatenate([_dot(xc2[:, n * bw:(n + 1) * bw], wx_ref[n]) for n in range(LRU_BLOCKS)], axis=-1)
    r = _sigmoid(ra + ba_ref[...])
    i = _sigmoid(rx + bx_ref[...])
    log_a = -LRU_C * r * _softplus(-lam_ref[...])
    a = jnp.exp(log_a).reshape(bb, tt, LRU_WIDTH)
    nem1 = -jnp.tanh(log_a) * (jnp.exp(2.0 * log_a) + 1.0)
    b = (jnp.sqrt(nem1) * (i * xc2)).reshape(bb, tt, LRU_WIDTH)
    s = 1
    while s < tt:
        abuf_ref[:, tt:2 * tt, :] = a
        bbuf_ref[:, tt:2 * tt, :] = b
        b = b + a * bbuf_ref[:, tt - s:2 * tt - s, :]
        a = a * abuf_ref[:, tt - s:2 * tt - s, :]
        s *= 2
    h = a * hcar_ref[...] + b
    hl = h[:, tt - 1:tt, :]
    hcar_ref[...] = hl
    hl_ref[...] = hl
    g = gate_ref[...]
    gelu = 0.5 * g * (1.0 + jnp.tanh(math.sqrt(2.0 / math.pi) * (g + 0.044715 * (g * g * g))))
    y_ref[...] = h * gelu


def lru_mixer(u3, h0, conv0, cw, cb, wa, ba, wx, bx, lam, bb, tt):
    nb, t, _ = u3.shape
    w = LRU_WIDTH
    full = lambda a: pl.BlockSpec(a.shape, lambda b, i: (0,) * a.ndim)
    blk = lambda c: pl.BlockSpec((bb, tt, w), lambda b, i: (b, i, c // w))
    st = lambda r: pl.BlockSpec((bb, r, w), lambda b, i: (b, 0, 0))
    return pl.pallas_call(
        functools.partial(_lru_kernel, tt=tt),
        grid=(nb // bb, t // tt),
        in_specs=[blk(C_LX), blk(C_LG), st(CONV_W - 1), st(1), full(cw), full(cb), full(wa), full(ba), full(wx),
                  full(bx), full(lam)],
        out_specs=[pl.BlockSpec((bb, tt, w), lambda b, i: (b, i, 0)), st(1), st(CONV_W - 1)],
        out_shape=[jax.ShapeDtypeStruct((nb, t, w), F32), jax.ShapeDtypeStruct((nb, 1, w), F32),
                   jax.ShapeDtypeStruct((nb, CONV_W - 1, w), F32)],
        scratch_shapes=[pltpu.VMEM((bb, SUBLANE + tt, w), F32), pltpu.VMEM((bb, 1, w), F32),
                        pltpu.VMEM((bb, 2 * tt, w), F32), pltpu.VMEM((bb, 2 * tt, w), F32)],
        compiler_params=_cp(("parallel", "arbitrary")),
        name="lru_mixer",
    )(u3, u3, conv0, h0, cw, cb, wa, ba, wx, bx, lam)


def _gdn_pre_kernel(xq_ref, xk_ref, xv_ref, gin_ref, cq_ref, ck_ref, cv_ref, wq_ref, wk_ref, wv_ref, alog_ref, dtb_ref,
                    q_ref, k_ref, v_ref, bg_ref, nq_ref, nk_ref, nv_ref, bq_ref, bk_ref, bv_ref, *, tt):
    groups = ((xq_ref, cq_ref, wq_ref, bq_ref, nq_ref, q_ref, GDN_DK ** -0.5),
              (xk_ref, ck_ref, wk_ref, bk_ref, nk_ref, k_ref, 1.0),
              (xv_ref, cv_ref, wv_ref, bv_ref, nv_ref, v_ref, None))
    for x_ref, c0_ref, w_ref, xbuf_ref, cnew_ref, o_ref, mult in groups:
        y = _causal_conv(x_ref, c0_ref, w_ref, xbuf_ref, cnew_ref, tt)
        y = y * _sigmoid(y)
        if mult is None:
            o_ref[...] = y
        else:
            for h in range(GDN_HEADS):
                yh = y[:, :, h * LANE:(h + 1) * LANE]
                yn = yh * lax.rsqrt(jnp.sum(yh * yh, axis=-1, keepdims=True) + NORM_EPS)
                o_ref[:, :, h * LANE:(h + 1) * LANE] = yn * mult if mult != 1.0 else yn
    x = gin_ref[...]
    lane = lax.broadcasted_iota(jnp.int32, x.shape, 2)
    g = -jnp.exp(alog_ref[...]) * _softplus(x + dtb_ref[...])
    bg_ref[...] = jnp.where(lane < GDN_HEADS, _sigmoid(x), jnp.where(lane < 2 * GDN_HEADS, g, 0.0))


def gdn_pre(u3, conv0, cw, alog, dtb, bb, tt):
    nb, t, _ = u3.shape
    w = GDN_HEADS * GDN_DK
    full = lambda a: pl.BlockSpec(a.shape, lambda b, i: (0,) * a.ndim)
    blk = lambda c: pl.BlockSpec((bb, tt, w), lambda b, i: (b, i, c // w))
    st = lambda k: pl.BlockSpec((bb, CONV_W - 1, w), lambda b, i: (b, 0, k))
    wsp = lambda k: pl.BlockSpec((CONV_W, w), lambda b, i: (0, k))
    oblk = pl.BlockSpec((bb, tt, w), lambda b, i: (b, i, 0))
    ost = pl.BlockSpec((bb, CONV_W - 1, w), lambda b, i: (b, 0, 0))
    return pl.pallas_call(
        functools.partial(_gdn_pre_kernel, tt=tt),
        grid=(nb // bb, t // tt),
        in_specs=[blk(C_GQ), blk(C_GQ + w), blk(C_GQ + 2 * w),
                  pl.BlockSpec((bb, tt, LANE), lambda b, i: (b, i, C_GB // LANE)),
                  st(0), st(1), st(2), wsp(0), wsp(1), wsp(2), full(alog), full(dtb)],
        out_specs=[oblk, oblk, oblk, pl.BlockSpec((bb, tt, LANE), lambda b, i: (b, i, 0)), ost, ost, ost],
        out_shape=[jax.ShapeDtypeStruct((nb, t, w), F32)] * 3 + [jax.ShapeDtypeStruct((nb, t, LANE), F32)]
        + [jax.ShapeDtypeStruct((nb, CONV_W - 1, w), F32)] * 3,
        scratch_shapes=[pltpu.VMEM((bb, SUBLANE + tt, w), F32)] * 3,
        compiler_params=_cp(("parallel", "arbitrary")),
        name="gdn_pre",
    )(u3, u3, u3, u3, conv0, conv0, conv0, cw, cw, cw, alog, dtb)


def _gdn_chunk_kernel(q_ref, k_ref, v_ref, bg_ref, z_ref, s0_ref, ng_ref, o_ref, s_ref, *, c, nchunk):
    bb = q_ref.shape[0]
    nblk = bb * GDN_HEADS
    r = nblk * c

    @pl.when(pl.program_id(1) == 0)
    def _():
        s_ref[...] = s0_ref[...]

    ri = lax.broadcasted_iota(jnp.int32, (r, r), 0)
    ci = lax.broadcasted_iota(jnp.int32, (r, r), 1)
    same = (ri // c) == (ci // c)
    causal = jnp.logical_and(same, ri >= ci)
    strict = jnp.logical_and(same, ri > ci)
    last = jnp.logical_and(same, lax.rem(ci, c) == c - 1)
    eye = (ri == ci).astype(F32)

    def stack(ref, rows, lanes=None):
        return jnp.concatenate([ref[b, rows, h * LANE:(h + 1) * LANE] if lanes is None else ref[b, rows, lanes + h:lanes + h + 1]
                                for b in range(bb) for h in range(GDN_HEADS)], axis=0)

    for g in range(nchunk):
        rows = slice(g * c, (g + 1) * c)
        q, k, v, z = stack(q_ref, rows), stack(k_ref, rows), stack(v_ref, rows), stack(z_ref, rows)
        beta, glog = stack(bg_ref, rows, 0), stack(bg_ref, rows, GDN_HEADS)
        gb = jnp.broadcast_to(glog, (r, LANE))
        g_hi = gb.astype(BF16)
        g_mid, g_lo = _split(gb - g_hi.astype(F32))
        cb = causal.astype(BF16)
        gc = (jnp.dot(cb, g_hi, preferred_element_type=F32) + (jnp.dot(cb, g_mid, preferred_element_type=F32)
              + jnp.dot(cb, g_lo, preferred_element_type=F32)))[:, 0:1]
        gc_row = jnp.sum(eye * gc, axis=0, keepdims=True)
        decay = jnp.where(causal, jnp.exp(jnp.where(causal, gc - gc_row, 0.0)), 0.0)
        g_last = jnp.sum(jnp.where(last, gc_row, 0.0), axis=1, keepdims=True)
        kb = k * beta
        lower = jnp.where(strict, _dot_nt(kb, k) * decay, 0.0)
        inv = eye - lower
        lsp = _split(lower)
        pw = _dot3(lsp, lsp)
        n = 2
        while n < c:
            psp = _split(pw)
            inv = inv + _dot3(_split(inv), psp)
            n *= 2
            if n < c:
                pw = _dot3(psp, psp)
        uw = _dot3(_split(inv), _split(jnp.concatenate([v * beta, kb * jnp.exp(gc)], axis=1)))
        uu, ww = uw[:, 0:LANE], uw[:, LANE:2 * LANE]
        qk = jnp.where(causal, _dot_nt(q, k) * decay, 0.0)
        qg = q * jnp.exp(gc)
        kd = k * jnp.exp(g_last - gc)
        eg = jnp.exp(g_last)
        ws, qs = [], []
        for blk in range(nblk):
            br = slice(blk * c, (blk + 1) * c)
            both = _dot(jnp.concatenate([ww[br], qg[br]], axis=0), s_ref[blk // GDN_HEADS, blk % GDN_HEADS])
            ws.append(both[0:c])
            qs.append(both[c:2 * c])
        v_new = uu - jnp.concatenate(ws, axis=0)
        o = jnp.concatenate(qs, axis=0) + _dot(qk, v_new)
        o = _rms(o, ng_ref[...]) * (z * _sigmoid(z))
        for blk in range(nblk):
            br = slice(blk * c, (blk + 1) * c)
            b, h = blk // GDN_HEADS, blk % GDN_HEADS
            s_ref[b, h] = s_ref[b, h] * eg[blk * c:blk * c + 1, :] + lax.dot_general(
                kd[br].astype(BF16), v_new[br].astype(BF16), (((0,), (0,)), ((), ())), preferred_element_type=F32)
            o_ref[b, rows, h * LANE:(h + 1) * LANE] = o[br]


def gdn_chunk(q, k, v, bg, u3, s0, ng, c, bb, nchunk):
    nb, t, w = q.shape
    tt = c * nchunk
    blk = pl.BlockSpec((bb, tt, w), lambda b, n: (b, n, 0))
    sblk = pl.BlockSpec((bb, GDN_HEADS, GDN_DK, GDN_DV), lambda b, n: (b, 0, 0, 0))
    return pl.pallas_call(
        functools.partial(_gdn_chunk_kernel, c=c, nchunk=nchunk),
        grid=(nb // bb, t // tt),
        in_specs=[blk, blk, blk, pl.BlockSpec((bb, tt, LANE), lambda b, n: (b, n, 0)),
                  pl.BlockSpec((bb, tt, w), lambda b, n: (b, n, C_GZ // w)), sblk,
                  pl.BlockSpec(ng.shape, lambda b, n: (0, 0))],
        out_specs=[blk, sblk],
        out_shape=[jax.ShapeDtypeStruct((nb, t, w), F32), jax.ShapeDtypeStruct(s0.shape, F32)],
        compiler_params=_cp(("parallel", "arbitrary")),
        name="gdn_chunk",
    )(q, k, v, bg, u3, s0, ng)


def _paged_attn_kernel(pt_ref, *refs, kind, layer, pp, nc, npages, tnew, scale):
    nheads = MLA_HEADS
    rows = nheads * tnew
    if kind == "mla":
        (q_ref, kn_ref, rn_ref, wuv_ref, ck_hbm, cr_hbm, o_ref,
         kbuf, rbuf, sem, m_ref, l_ref, acc_ref) = refs
        streams = ((ck_hbm, kbuf), (cr_hbm, rbuf))
    else:
        (q_ref, kn_ref, vn_ref, lfn_ref, ck_hbm, cv_hbm, clf_hbm, o_ref,
         kbuf, vbuf, sem, lfbuf, lfsem, cum_ref, tot_ref, cq_ref, m_ref, l_ref, acc_ref) = refs
        streams = ((ck_hbm, kbuf), (cv_hbm, vbuf))
    b, c = pl.program_id(0), pl.program_id(1)
    nb = pl.num_programs(0)
    step = b * nc + c
    slot = lax.rem(step, 2)

    def page_copies(bb, cc, sl):
        out = []
        for p in range(pp):
            page = pt_ref[bb * npages + cc * pp + p]
            for si, (hbm, buf) in enumerate(streams):
                out.append(pltpu.make_async_copy(hbm.at[layer, page], buf.at[sl, p], sem.at[sl, si]))
        return out

    def logf_copies(bb, sl):
        return [pltpu.make_async_copy(clf_hbm.at[layer, pt_ref[bb * npages + p]],
                                      lfbuf.at[sl, pl.ds(FOX_HEADS * p, FOX_HEADS)], lfsem.at[sl]) for p in range(npages)]

    @pl.when(step == 0)
    def _():
        for cp in page_copies(0, 0, 0):
            cp.start()
        if kind == "fox":
            for cp in logf_copies(0, 0):
                cp.start()

    @pl.when(step + 1 < nb * nc)
    def _():
        nxt = step + 1
        for cp in page_copies(nxt // nc, lax.rem(nxt, nc), 1 - slot):
            cp.start()

    if kind == "fox":
        @pl.when(jnp.logical_and(c == 0, b + 1 < nb))
        def _():
            for cp in logf_copies(b + 1, 1 - lax.rem(b, 2)):
                cp.start()

    q = q_ref[...].reshape(rows, q_ref.shape[-1])
    rowq = lax.rem(lax.broadcasted_iota(jnp.int32, (rows, LANE), 0), tnew)
    lanei = lax.broadcasted_iota(jnp.int32, (rows, LANE), 1)

    @pl.when(c == 0)
    def _():
        m_ref[...] = jnp.full(m_ref.shape, -jnp.inf, F32)
        l_ref[...] = jnp.zeros_like(l_ref)
        acc_ref[...] = jnp.zeros_like(acc_ref)
        if kind == "fox":
            bs = lax.rem(b, 2)
            for cp in logf_copies(b, bs):
                cp.wait()
            x = lfbuf[bs]
            ti = lax.broadcasted_iota(jnp.int32, (PAGE, PAGE), 0)
            tj = lax.broadcasted_iota(jnp.int32, (PAGE, PAGE), 1)
            within = _dot_hi(x, (ti <= tj).astype(F32))
            tot = jnp.broadcast_to(within[:, PAGE - 1:PAGE], within.shape)
            n4 = npages * FOX_HEADS
            rowi = lax.broadcasted_iota(jnp.int32, (n4, PAGE), 0)
            incl = tot
            sh = FOX_HEADS
            while sh < n4:
                incl = incl + jnp.where(rowi >= sh, pltpu.roll(incl, sh, 0), 0.0)
                sh *= 2
            offs = incl - tot
            cum_ref[...] = (within + offs).reshape(cum_ref.shape)
            total = offs[n4 - FOX_HEADS:n4, 0:1] + tot[n4 - FOX_HEADS:n4, 0:1]
            lfn = lfn_ref[...]
            ncum = lfn
            sh = 1
            while sh < tnew:
                ncum = ncum + jnp.where(lax.broadcasted_iota(jnp.int32, ncum.shape, 0) >= sh, pltpu.roll(ncum, sh, 0), 0.0)
                sh *= 2
            for h in range(nheads):
                cn = total[h:h + 1, :] + ncum[:, h:h + 1]
                cq_ref[h * tnew:(h + 1) * tnew, :] = cn
                eye = (lax.broadcasted_iota(jnp.int32, (tnew, LANE), 0) == lax.broadcasted_iota(jnp.int32, (tnew, LANE), 1))
                tot_ref[h:h + 1, :] = jnp.sum(jnp.where(eye, cn, 0.0), axis=0, keepdims=True)

    for cp in page_copies(b, c, slot):
        cp.wait()

    def online_update(s, v):
        m_prev = m_ref[...]
        m_new = jnp.maximum(m_prev, jnp.max(s, axis=-1, keepdims=True))
        alpha = jnp.exp(m_prev - m_new)
        p = jnp.exp(s - m_new)
        l_ref[...] = alpha * l_ref[...] + jnp.sum(p, axis=-1, keepdims=True)
        acc_ref[...] = alpha * acc_ref[...] + jnp.dot(p.astype(BF16), v, preferred_element_type=F32)
        m_ref[...] = m_new

    keys = kbuf[slot].reshape(pp * PAGE, kbuf.shape[-1]).astype(BF16)
    if kind == "mla":
        qpe = q[:, LANE:LANE + MLA_ROPE].astype(BF16)
        s_rope = jnp.concatenate([jnp.dot(qpe, rbuf[slot, p].astype(BF16), preferred_element_type=F32) for p in range(pp)],
                                 axis=1)
        s = (_dot_nt(q[:, 0:LANE], keys) + s_rope) * scale
        online_update(s, keys)
    else:
        vals = vbuf[slot].reshape(pp * PAGE, FOX_HD).astype(BF16)
        s = _dot_nt(q, keys) * scale
        ck = jnp.concatenate(
            [jnp.concatenate([jnp.broadcast_to(cum_ref[c, FOX_HEADS * p + h:FOX_HEADS * p + h + 1, :], (tnew, PAGE))
                              for h in range(nheads)], axis=0) for p in range(pp)], axis=1)
        online_update(s + (cq_ref[...] - ck), vals)

    @pl.when(c == nc - 1)
    def _():
        zpad = jnp.zeros((LANE - tnew, LANE), F32)
        kn = jnp.concatenate([kn_ref[...], zpad], axis=0)
        if kind == "mla":
            rn = jnp.concatenate([rn_ref[...], jnp.zeros((LANE - tnew, MLA_ROPE), F32)], axis=0)
            s = (_dot_nt(q[:, 0:LANE], kn) + _dot_nt(q[:, LANE:LANE + MLA_ROPE], rn)) * scale
            vn = kn
        else:
            vn = jnp.concatenate([vn_ref[...], zpad], axis=0)
            s = _dot_nt(q, kn) * scale
            ckn = jnp.concatenate([jnp.broadcast_to(tot_ref[h:h + 1, :], (tnew, LANE)) for h in range(nheads)], axis=0)
            s = s + (cq_ref[...] - ckn)
        s = jnp.where(lanei <= rowq, s, -jnp.inf)
        online_update(s, vn.astype(BF16))
        o = acc_ref[...] / l_ref[...]
        for h in range(nheads):
            oh = o[h * tnew:(h + 1) * tnew, :]
            o_ref[:, h * LANE:(h + 1) * LANE] = _dot(oh, wuv_ref[h]) if kind == "mla" else oh


def paged_attn(kind, layer, page_table, q, new_parts, caches, wuv, scale, pp):
    nheads, m, dq = q.shape
    nbatch, npages = page_table.shape
    tnew = m // nbatch
    nc = npages // pp
    rows = nheads * tnew
    qspec = pl.BlockSpec((nheads, tnew, dq), lambda b, c, pt: (0, b, 0))
    newspec = lambda a: pl.BlockSpec((tnew, a.shape[1]), lambda b, c, pt: (b, 0))
    anyspec = pl.BlockSpec(memory_space=pl.ANY)
    in_specs = [qspec] + [newspec(a) for a in new_parts]
    args = [q] + list(new_parts)
    if kind == "mla":
        in_specs.append(pl.BlockSpec(wuv.shape, lambda b, c, pt: (0, 0, 0)))
        args.append(wuv)
        scratch = [pltpu.VMEM((2, pp, PAGE, MLA_KV_LORA), F32), pltpu.VMEM((2, pp, MLA_ROPE, PAGE), F32),
                   pltpu.SemaphoreType.DMA((2, 2))]
    else:
        scratch = [pltpu.VMEM((2, pp, PAGE, FOX_HD), F32), pltpu.VMEM((2, pp, PAGE, FOX_HD), F32),
                   pltpu.SemaphoreType.DMA((2, 2)),
                   pltpu.VMEM((2, npages * FOX_HEADS, PAGE), F32), pltpu.SemaphoreType.DMA((2,)),
                   pltpu.VMEM((nc, pp * FOX_HEADS, PAGE), F32), pltpu.VMEM((SUBLANE, LANE), F32),
                   pltpu.VMEM((rows, 1), F32)]
    in_specs += [anyspec] * len(caches)
    args += list(caches)
    scratch += [pltpu.VMEM((rows, 1), F32), pltpu.VMEM((rows, 1), F32), pltpu.VMEM((rows, LANE), F32)]
    return pl.pallas_call(
        functools.partial(_paged_attn_kernel, kind=kind, layer=layer, pp=pp, nc=nc, npages=npages, tnew=tnew, scale=scale),
        grid_spec=pltpu.PrefetchScalarGridSpec(
            num_scalar_prefetch=1, grid=(nbatch, nc), in_specs=in_specs,
            out_specs=pl.BlockSpec((tnew, nheads * LANE), lambda b, c, pt: (b, 0)),
            scratch_shapes=scratch),
        out_shape=jax.ShapeDtypeStruct((m, nheads * LANE), F32),
        compiler_params=_cp(("arbitrary", "arbitrary")),
        name="paged_attn_" + kind,
    )(page_table.reshape(-1), *args)


def _pad_lanes(a, width, offset=0):
    a = a.reshape(1, -1)
    return jnp.pad(a, ((0, 0), (offset, width - offset - a.shape[1])))


def _layer_params(l, w_in, mla_q_a_g, mla_w_uq, mla_q_norm_g, mla_kv_norm_g, mla_kpe_norm_g, mla_w_uk, mla_w_uv,
                  fox_q_norm_g, fox_k_norm_g, fox_f_bias, lru_conv_w, lru_conv_b, lru_w_a, lru_b_a, lru_w_x, lru_b_x,
                  lru_lambda, gdn_conv_w, gdn_a_log, gdn_dt_bias, gdn_norm_g, w_out, w_gate, w_up, w_down):
    w = w_in[l]
    d = w.shape[0]
    z = lambda n: jnp.zeros((d, n), w.dtype)
    n_qkv = MLA_Q_LORA + MLA_KV_LORA
    n_mla = n_qkv + MLA_ROPE
    n_fq = FOX_HEADS * FOX_HD
    n_fox = n_fq + 2 * FOX_HD + FOX_HEADS
    n_main = w.shape[1] - 2 * GDN_HEADS
    w_in_p = jnp.concatenate([w[:, :n_qkv], w[:, n_mla:n_mla + n_fq], w[:, n_qkv:n_mla], z(LANE - MLA_ROPE),
                              w[:, n_mla + n_fq:n_mla + n_fox], z(LANE - FOX_HEADS),
                              w[:, n_mla + n_fox:n_main], w[:, n_main:], z(D_INP - C_GB - 2 * GDN_HEADS)], axis=1).astype(BF16)
    uq = mla_w_uq[l]
    wuq = jnp.concatenate([uq[:, :, :MLA_NOPE].reshape(MLA_Q_LORA, -1),
                           jnp.pad(uq[:, :, MLA_NOPE:], ((0, 0), (0, 0), (0, LANE - MLA_ROPE))).reshape(MLA_Q_LORA, -1)],
                          axis=1).astype(BF16)
    return dict(
        w_in=w_in_p,
        gqa=mla_q_a_g[l].reshape(1, -1), wuq=wuq,
        gqn=mla_q_norm_g[l, :MLA_NOPE].reshape(1, -1), gqr=_pad_lanes(mla_q_norm_g[l, MLA_NOPE:], LANE),
        gkv=mla_kv_norm_g[l].reshape(1, -1), gkpe=_pad_lanes(mla_kpe_norm_g[l], LANE),
        wuk=jnp.transpose(mla_w_uk[l], (1, 2, 0)).astype(BF16), wuv=jnp.transpose(mla_w_uv[l], (1, 0, 2)).astype(BF16),
        fgq=fox_q_norm_g[l].reshape(1, -1), fgk=fox_k_norm_g[l].reshape(1, -1), fb=_pad_lanes(fox_f_bias[l], LANE),
        lcw=lru_conv_w[l], lcb=lru_conv_b[l].reshape(1, -1), lwa=lru_w_a[l].astype(BF16), lba=lru_b_a[l].reshape(1, -1),
        lwx=lru_w_x[l].astype(BF16), lbx=lru_b_x[l].reshape(1, -1), lam=lru_lambda[l].reshape(1, -1),
        gcw=gdn_conv_w[l], alog=_pad_lanes(gdn_a_log[l], LANE, GDN_HEADS), dtb=_pad_lanes(gdn_dt_bias[l], LANE, GDN_HEADS),
        gng=gdn_norm_g[l].reshape(1, -1),
        w_out=w_out, w_gate=w_gate, w_up=w_up, w_down=w_down,
    )


def _tiles(t):
    if t >= 512:
        return dict(tm=512, tm_mm=1024, bb=1, tt=512, tq=256, tk=512, gdn_bb=1, gdn_chunks=4)
    return dict(tm=512, tm_mm=512, bb=512 // t, tt=t, tq=None, tk=None, gdn_bb=8, gdn_chunks=1)


def _trunk_layer(l, p, x, mods, nb, t, g_mix, g_ffn, tabs, past):
    sh1, sc1, g1, sh2, sc2, g2 = mods
    m = nb * t
    ti = _tiles(t)
    tm = ti["tm"]
    u = norm_mod_matmul(x, g_mix, sc1, sh1, p["w_in"], t, ti["tm_mm"], 1024)
    u3 = u.reshape(nb, t, D_INP)
    q, lat, rot, kfull = mla_pre(u, tabs, tabs[0].shape[0] // tm, p["gqa"], p["wuq"], p["gqn"], p["gqr"], p["gkv"],
                                 p["gkpe"], p["wuk"], tm)
    mla_scale = (MLA_NOPE + MLA_ROPE) ** -0.5
    fox_scale = FOX_HD ** -0.5
    if past is None:
        o_lat = causal_attn(q, kfull, None, None, None, nb, mla_scale, MLA_KV_LORA, ti["tq"], ti["tk"], False)
        o_mla = head_proj(o_lat, p["wuv"], tm)
        fq, fk, fv, lf, ccol, crow = fox_pre(u, p["fgq"], p["fgk"], p["fb"], nb, tm, True)
        o_fox = causal_attn(fq, fk, fv, ccol, crow, nb, fox_scale, FOX_HD, ti["tq"], ti["tk"], True)
        h0 = jnp.zeros((nb, 1, LRU_WIDTH), F32)
        lconv0 = jnp.zeros((nb, CONV_W - 1, LRU_WIDTH), F32)
        s0 = jnp.zeros((nb, GDN_HEADS, GDN_DK, GDN_DV), F32)
        gconv0 = jnp.zeros((nb, CONV_W - 1, 3 * GDN_HEADS * GDN_DK), F32)
    else:
        page_table, c_lat, c_rope, c_k, c_v, c_lf, h0, lconv0, s0, gconv0 = past
        pp = min(PAGES_PER_STEP, page_table.shape[1])
        o_mla = paged_attn("mla", l, page_table, q, [lat, rot], [c_lat, c_rope], p["wuv"], mla_scale, pp)
        fq, fk, fv, lf = fox_pre(u, p["fgq"], p["fgk"], p["fb"], 1, tm, False)
        lf_pad = jnp.pad(lf, ((0, 0), (0, LANE - FOX_HEADS)))
        o_fox = paged_attn("fox", l, page_table, fq, [fk, fv, lf_pad], [c_k, c_v, c_lf], None, fox_scale, pp)
        h0 = h0[:, None, :]
    y_lru, h_new, lconv_new = lru_mixer(u3, h0, lconv0, p["lcw"], p["lcb"], p["lwa"], p["lba"], p["lwx"], p["lbx"],
                                        p["lam"], ti["bb"], ti["tt"])
    gq, gk, gv, bg, nq, nk, nv = gdn_pre(u3, gconv0, p["gcw"], p["alog"], p["dtb"], ti["bb"], ti["tt"])
    o_gdn, s_new = gdn_chunk(gq, gk, gv, bg, u3, s0, p["gng"], math.gcd(t, GDN_CHUNK), ti["gdn_bb"], ti["gdn_chunks"])
    w = MLA_HEADS * LANE
    x = matmul_residual([o_mla, o_fox, y_lru.reshape(m, w), o_gdn.reshape(m, w)], p["w_out"], l, x, g1, t, ti["tm_mm"], 512)
    a = ffn_up(x, g_ffn, sc2, sh2, p["w_gate"], p["w_up"], l, t, ti["tm_mm"], 512)
    x = matmul_residual([a], p["w_down"], l, x, g2, t, 512, 512)
    states = (lat.reshape(nb, t, -1), rot.reshape(nb, t, -1), fk.reshape(nb, t, 1, FOX_HD), fv.reshape(nb, t, 1, FOX_HD),
              lf.reshape(nb, t, FOX_HEADS), h_new.reshape(nb, LRU_WIDTH), lconv_new, s_new,
              jnp.concatenate([nq, nk, nv], axis=-1))
    return x, states


def kernel(x_prompt, x_sample, cache_mla_lat, cache_mla_rope, cache_fox_k, cache_fox_v, cache_fox_logf, state_lru_h, state_lru_conv, state_gdn_S, state_gdn_conv, page_table, c_prompt, c_sample, w_ada, b_ada, norm_mix_g, w_in, mla_q_a_g, mla_w_uq, mla_q_norm_g, mla_kv_norm_g, mla_kpe_norm_g, mla_w_uk, mla_w_uv, fox_q_norm_g, fox_k_norm_g, fox_f_bias, lru_conv_w, lru_conv_b, lru_w_a, lru_b_a, lru_w_x, lru_b_x, lru_lambda, gdn_conv_w, gdn_a_log, gdn_dt_bias, gdn_norm_g, w_out, norm_ffn_g, w_gate, w_up, w_down):
    bp, tp, d = x_prompt.shape
    bs, ts, _ = x_sample.shape
    n_pool = cache_mla_lat.shape[1]
    n_past = page_table.shape[1] * PAGE
    c_k = cache_fox_k.reshape(DEPTH, n_pool, PAGE, FOX_HD)
    c_v = cache_fox_v.reshape(DEPTH, n_pool, PAGE, FOX_HD)
    c_rope = jnp.swapaxes(cache_mla_rope, 2, 3)
    c_lf = jnp.swapaxes(cache_fox_logf, 2, 3)
    tabs_p = rope_tables(jnp.arange(tp, dtype=jnp.int32))
    tm_s = _tiles(ts)["tm"]
    tabs_s = rope_tables(n_past + jnp.arange(tm_s, dtype=jnp.int32) % ts)
    c_all = jnp.concatenate([c_sample, c_prompt], axis=0)
    x_p = x_prompt.reshape(bp * tp, d)
    x_s = x_sample.reshape(bs * ts, d)
    states_p, states_s = [], []
    for l in range(DEPTH):
        p = _layer_params(l, w_in, mla_q_a_g, mla_w_uq, mla_q_norm_g, mla_kv_norm_g, mla_kpe_norm_g, mla_w_uk, mla_w_uv,
                          fox_q_norm_g, fox_k_norm_g, fox_f_bias, lru_conv_w, lru_conv_b, lru_w_a, lru_b_a, lru_w_x,
                          lru_b_x, lru_lambda, gdn_conv_w, gdn_a_log, gdn_dt_bias, gdn_norm_g, w_out, w_gate, w_up, w_down)
        mod = ada_mod(c_all, w_ada, b_ada, l)
        mods = jnp.split(mod, 6, axis=-1)
        mods_p = [mm[bs:, None, :] for mm in mods]
        mods_s = [jnp.repeat(mm[:bs], ts, axis=0) for mm in mods]
        x_p, st_p = _trunk_layer(l, p, x_p, mods_p, bp, tp, norm_mix_g[l], norm_ffn_g[l], tabs_p, None)
        past = (page_table, cache_mla_lat, c_rope, c_k, c_v, c_lf, state_lru_h[l], state_lru_conv[l],
                state_gdn_S[l], state_gdn_conv[l])
        x_s, st_s = _trunk_layer(l, p, x_s, mods_s, bs, ts, norm_mix_g[l], norm_ffn_g[l], tabs_s, past)
        states_p.append(st_p)
        states_s.append(st_s)
    sp = [jnp.stack(a, axis=0) for a in zip(*states_p)]
    ss = [jnp.stack(a, axis=0) for a in zip(*states_s)]
    out = [x_p.reshape(bp, tp, d), x_s.reshape(bs, ts, d)]
    for a, b in zip(sp, ss):
        out += [a, b]
    return tuple(out)
```

```python
import functools
import math

import jax
import jax.numpy as jnp
from jax import lax
from jax.experimental import pallas as pl
from jax.experimental.pallas import tpu as pltpu

F32 = jnp.float32
BF16 = jnp.bfloat16
HI = lax.Precision.HIGHEST

D_MODEL = 2048
DEPTH = 2
PAGE = 128
MLA_HEADS = 4
MLA_Q_LORA = 384
MLA_KV_LORA = 128
MLA_NOPE = 128
MLA_ROPE = 64
ROPE_THETA = 10000.0
FOX_HEADS = 4
FOX_HD = 128
LRU_WIDTH = 512
LRU_BLOCKS = 4
LRU_C = 8.0
GDN_HEADS = 4
GDN_DK = 128
GDN_DV = 128
GDN_CHUNK = 64
CONV_W = 4
NORM_EPS = 1e-6
LANE = 128
SUBLANE = 8
VMEM_LIMIT = 56 * 1024 * 1024
PAGES_PER_STEP = 64

C_QA, C_KVA, C_KPE = 0, 384, 1024
C_FQ, C_FK, C_FV, C_FL = 512, 1152, 1280, 1408
C_LX, C_LG = 1536, 2048
C_GQ, C_GZ, C_GB = 2560, 4096, 4608
D_INP = 5120


def _cp(sem):
    return pltpu.CompilerParams(dimension_semantics=sem, vmem_limit_bytes=VMEM_LIMIT)


def _sigmoid(x):
    return 1.0 / (1.0 + jnp.exp(-x))


def _softplus(x):
    return jnp.maximum(x, 0.0) + jnp.log1p(jnp.exp(-jnp.abs(x)))


def _dot(a, b):
    return jnp.dot(a.astype(BF16), b.astype(BF16), preferred_element_type=F32)


def _dot_nt(a, b):
    return lax.dot_general(a.astype(BF16), b.astype(BF16), (((1,), (1,)), ((), ())), preferred_element_type=F32)


def _dot_hi(a, b):
    return jnp.dot(a, b, precision=HI, preferred_element_type=F32)


def _split(a):
    hi = a.astype(BF16)
    return hi, (a - hi.astype(F32)).astype(BF16)


def _dot3(a, b):
    (ah, al), (bh, bl) = a, b
    d = lambda x, y: jnp.dot(x, y, preferred_element_type=F32)
    return d(ah, bh) + (d(ah, bl) + d(al, bh))


def _rms(x, g, n=None):
    n = x.shape[-1] if n is None else n
    ms = jnp.sum(x * x, axis=-1, keepdims=True) * (1.0 / n)
    return x * lax.rsqrt(ms + NORM_EPS) * g


def _ada_kernel(c_ref, w_ref, b_ref, o_ref):
    c = c_ref[...]
    o_ref[...] = _dot(c * _sigmoid(c), w_ref[...]) + b_ref[...]


def ada_mod(c_all, w_ada, b_ada, layer):
    m, d = c_all.shape
    n = w_ada.shape[-1]
    tn = 1024
    return pl.pallas_call(
        _ada_kernel,
        grid=(n // tn,),
        in_specs=[pl.BlockSpec((m, d), lambda j: (0, 0)),
                  pl.BlockSpec((None, d, tn), lambda j: (layer, 0, j)),
                  pl.BlockSpec((None, 1, tn), lambda j: (layer, 0, j))],
        out_specs=pl.BlockSpec((m, tn), lambda j: (0, j)),
        out_shape=jax.ShapeDtypeStruct((m, n), F32),
        compiler_params=_cp(("arbitrary",)),
        name="ada_mod",
    )(c_all, w_ada, b_ada.reshape(DEPTH, 1, n))


def _mod_spec(mod, tm, rows_per_group):
    if mod.ndim == 3:
        bpg = rows_per_group // tm
        return pl.BlockSpec((None, 1, mod.shape[-1]), lambda i, j: (i // bpg, 0, 0))
    return pl.BlockSpec((tm, mod.shape[-1]), lambda i, j: (i, 0))


def _norm_mod(x_ref, g_ref, sc_ref, sh_ref, h_ref):
    x = x_ref[...]
    y = x * lax.rsqrt(jnp.mean(x * x, axis=-1, keepdims=True) + NORM_EPS) * g_ref[...]
    h_ref[...] = (y * (1.0 + sc_ref[...]) + sh_ref[...]).astype(BF16)


def _nmm_kernel(x_ref, g_ref, sc_ref, sh_ref, w_ref, o_ref, h_ref):
    @pl.when(pl.program_id(1) == 0)
    def _():
        _norm_mod(x_ref, g_ref, sc_ref, sh_ref, h_ref)
    o_ref[...] = lax.dot_general(h_ref[...], w_ref[...], (((1,), (1,)), ((), ())), preferred_element_type=F32)


def norm_mod_matmul(x, g, scale, shift, w, rows_per_group, tm, tn):
    m, d = x.shape
    n = w.shape[0]
    return pl.pallas_call(
        _nmm_kernel,
        grid=(m // tm, n // tn),
        in_specs=[pl.BlockSpec((tm, d), lambda i, j: (i, 0)),
                  pl.BlockSpec((1, d), lambda i, j: (0, 0)),
                  _mod_spec(scale, tm, rows_per_group),
                  _mod_spec(shift, tm, rows_per_group),
                  pl.BlockSpec((tn, d), lambda i, j: (j, 0))],
        out_specs=pl.BlockSpec((tm, tn), lambda i, j: (i, j)),
        out_shape=jax.ShapeDtypeStruct((m, n), F32),
        scratch_shapes=[pltpu.VMEM((tm, d), BF16)],
        compiler_params=_cp(("parallel", "arbitrary")),
        name="norm_mod_matmul",
    )(x, g.reshape(1, d), scale, shift, w)


def _ffn_up_kernel(x_ref, g_ref, sc_ref, sh_ref, wg_ref, wu_ref, o_ref, h_ref):
    @pl.when(pl.program_id(1) == 0)
    def _():
        _norm_mod(x_ref, g_ref, sc_ref, sh_ref, h_ref)
    h = h_ref[...]
    a = jnp.dot(h, wg_ref[...].astype(BF16), preferred_element_type=F32)
    b = jnp.dot(h, wu_ref[...].astype(BF16), preferred_element_type=F32)
    o_ref[...] = (a * _sigmoid(a) * b).astype(BF16)


def ffn_up(x, g, scale, shift, wg, wu, layer, rows_per_group, tm, tn):
    m, d = x.shape
    n = wg.shape[2]
    return pl.pallas_call(
        _ffn_up_kernel,
        grid=(m // tm, n // tn),
        in_specs=[pl.BlockSpec((tm, d), lambda i, j: (i, 0)),
                  pl.BlockSpec((1, d), lambda i, j: (0, 0)),
                  _mod_spec(scale, tm, rows_per_group),
                  _mod_spec(shift, tm, rows_per_group),
                  pl.BlockSpec((None, d, tn), lambda i, j: (layer, 0, j)),
                  pl.BlockSpec((None, d, tn), lambda i, j: (layer, 0, j))],
        out_specs=pl.BlockSpec((tm, tn), lambda i, j: (i, j)),
        out_shape=jax.ShapeDtypeStruct((m, n), BF16),
        scratch_shapes=[pltpu.VMEM((tm, d), BF16)],
        compiler_params=_cp(("parallel", "arbitrary")),
        name="ffn_up",
    )(x, g.reshape(1, d), scale, shift, wg, wu)


def _mm_res_kernel(*refs, n_in):
    a_refs, w_refs = refs[:n_in], refs[n_in:2 * n_in]
    x_ref, gate_ref, o_ref = refs[2 * n_in:]
    acc = _dot(a_refs[0][...], w_refs[0][...])
    for a_ref, w_ref in zip(a_refs[1:], w_refs[1:]):
        acc = acc + _dot(a_ref[...], w_ref[...])
    o_ref[...] = x_ref[...] + gate_ref[...] * acc


def matmul_residual(a_list, w, layer, x, gate, rows_per_group, tm, tn):
    m, d = x.shape
    n_in = len(a_list)
    kk = a_list[0].shape[1]
    in_specs = [pl.BlockSpec((tm, kk), lambda i, j: (i, 0)) for _ in a_list]
    in_specs += [pl.BlockSpec((None, kk, tn), lambda i, j, k=k: (layer, k, j)) for k in range(n_in)]
    in_specs += [pl.BlockSpec((tm, tn), lambda i, j: (i, j))]
    if gate.ndim == 3:
        bpg = rows_per_group // tm
        in_specs += [pl.BlockSpec((None, 1, tn), lambda i, j: (i // bpg, 0, j))]
    else:
        in_specs += [pl.BlockSpec((tm, tn), lambda i, j: (i, j))]
    return pl.pallas_call(
        functools.partial(_mm_res_kernel, n_in=n_in),
        grid=(m // tm, d // tn),
        in_specs=in_specs,
        out_specs=pl.BlockSpec((tm, tn), lambda i, j: (i, j)),
        out_shape=jax.ShapeDtypeStruct((m, d), F32),
        compiler_params=_cp(("parallel", "arbitrary")),
        name="matmul_residual",
    )(*a_list, *([w] * n_in), x, gate)


def rope_tables(pos):
    half = MLA_ROPE // 2
    freqs = ROPE_THETA ** (-jnp.arange(half, dtype=F32) / half)
    ang = pos.astype(F32)[:, None] * freqs[None, :]
    cos, sin = jnp.cos(ang), jnp.sin(ang)
    z = jnp.zeros_like(cos)
    tc = jnp.concatenate([cos, cos, z, z], axis=-1)
    ts1 = jnp.concatenate([-sin, z, z, z], axis=-1)
    ts2 = jnp.concatenate([z, sin, z, z], axis=-1)
    return tc, ts1, ts2


def _rope(x, tc, ts1, ts2):
    return x * tc + pltpu.roll(x, LANE - MLA_ROPE // 2, 1) * ts1 + pltpu.roll(x, MLA_ROPE // 2, 1) * ts2


def _mla_pre_kernel(qa_ref, kva_ref, kpe_ref, tc_ref, ts1_ref, ts2_ref, gqa_ref, wuq_ref, gqn_ref, gqr_ref,
                    gkv_ref, gkpe_ref, wuk_ref, q_ref, lat_ref, rot_ref, kf_ref):
    tc, ts1, ts2 = tc_ref[...], ts1_ref[...], ts2_ref[...]
    qn = _rms(qa_ref[...], gqa_ref[...])
    q = _dot(qn, wuq_ref[...])
    hw = MLA_HEADS * MLA_NOPE
    for h in range(MLA_HEADS):
        nope = q[:, h * LANE:(h + 1) * LANE]
        rope = q[:, hw + h * LANE:hw + (h + 1) * LANE]
        ms = (jnp.sum(nope * nope, axis=-1, keepdims=True) + jnp.sum(rope * rope, axis=-1, keepdims=True)) * (
            1.0 / (MLA_NOPE + MLA_ROPE))
        r = lax.rsqrt(ms + NORM_EPS)
        q_ref[h, :, 0:LANE] = _dot(nope * r * gqn_ref[...], wuk_ref[h])
        q_ref[h, :, LANE:2 * LANE] = _rope(rope * r * gqr_ref[...], tc, ts1, ts2)
    lat = _rms(kva_ref[...], gkv_ref[...])
    rot = _rope(_rms(kpe_ref[...], gkpe_ref[...], MLA_ROPE), tc, ts1, ts2)
    lat_ref[...] = lat
    rot_ref[...] = rot[:, :MLA_ROPE]
    kf_ref[:, 0:LANE] = lat.astype(BF16)
    kf_ref[:, LANE:2 * LANE] = rot.astype(BF16)


def mla_pre(u, tabs, tab_blocks, gqa, wuq, gqn, gqr, gkv, gkpe, wuk, tm):
    m = u.shape[0]
    col = lambda c, w: pl.BlockSpec((tm, w), lambda i: (i, c // w))
    tab = pl.BlockSpec((tm, LANE), lambda i: (i % tab_blocks, 0))
    full = lambda a: pl.BlockSpec(a.shape, lambda i: (0,) * a.ndim)
    return pl.pallas_call(
        _mla_pre_kernel,
        grid=(m // tm,),
        in_specs=[col(C_QA, MLA_Q_LORA), col(C_KVA, LANE), col(C_KPE, LANE), tab, tab, tab,
                  full(gqa), full(wuq), full(gqn), full(gqr), full(gkv), full(gkpe), full(wuk)],
        out_specs=[pl.BlockSpec((MLA_HEADS, tm, 2 * LANE), lambda i: (0, i, 0)),
                   pl.BlockSpec((tm, LANE), lambda i: (i, 0)),
                   pl.BlockSpec((tm, MLA_ROPE), lambda i: (i, 0)),
                   pl.BlockSpec((tm, 2 * LANE), lambda i: (i, 0))],
        out_shape=[jax.ShapeDtypeStruct((MLA_HEADS, m, 2 * LANE), F32),
                   jax.ShapeDtypeStruct((m, LANE), F32),
                   jax.ShapeDtypeStruct((m, MLA_ROPE), F32),
                   jax.ShapeDtypeStruct((m, 2 * LANE), BF16)],
        compiler_params=_cp(("parallel",)),
        name="mla_pre",
    )(u, u, u, *tabs, gqa, wuq, gqn, gqr, gkv, gkpe, wuk)


def _head_proj_kernel(o_ref, w_ref, y_ref):
    for h in range(MLA_HEADS):
        y_ref[:, h * LANE:(h + 1) * LANE] = _dot(o_ref[h], w_ref[h])


def head_proj(o, w, tm):
    m = o.shape[1]
    return pl.pallas_call(
        _head_proj_kernel,
        grid=(m // tm,),
        in_specs=[pl.BlockSpec((MLA_HEADS, tm, LANE), lambda i: (0, i, 0)),
                  pl.BlockSpec(w.shape, lambda i: (0, 0, 0))],
        out_specs=pl.BlockSpec((tm, MLA_HEADS * LANE), lambda i: (i, 0)),
        out_shape=jax.ShapeDtypeStruct((m, MLA_HEADS * LANE), F32),
        compiler_params=_cp(("parallel",)),
        name="head_proj",
    )(o, w)


def _fox_pre_kernel(q_ref, k_ref, v_ref, f_ref, gq_ref, gk_ref, fb_ref, qo_ref, ko_ref, vo_ref, lf_ref, *rest, with_cum):
    for h in range(FOX_HEADS):
        qo_ref[h] = _rms(q_ref[:, h * LANE:(h + 1) * LANE], gq_ref[...])
    ko_ref[...] = _rms(k_ref[...], gk_ref[...])
    vo_ref[...] = v_ref[...]
    z = f_ref[...] + fb_ref[...]
    logf = jnp.minimum(z, 0.0) - jnp.log1p(jnp.exp(-jnp.abs(z)))
    lane = lax.broadcasted_iota(jnp.int32, logf.shape, 1)
    logf = jnp.where(lane < FOX_HEADS, logf, 0.0)
    lf_ref[...] = logf[:, :FOX_HEADS]
    if with_cum:
        cc_ref, cr_ref, carry_ref = rest
        tm = logf.shape[0]

        @pl.when(pl.program_id(1) == 0)
        def _():
            carry_ref[...] = jnp.zeros_like(carry_ref)
        tri = (lax.broadcasted_iota(jnp.int32, (tm, tm), 0) >= lax.broadcasted_iota(jnp.int32, (tm, tm), 1)).astype(F32)
        cum = _dot_hi(tri, logf) + carry_ref[...]
        carry_ref[...] = cum[tm - 1:tm, :]
        cc_ref[...] = cum[:, :FOX_HEADS]
        cr_ref[...] = cum.T[:FOX_HEADS, :]


def fox_pre(u, gq, gk, fb, nb, tm, with_cum):
    m = u.shape[0]
    t = m // nb
    nt = t // tm
    col = lambda c, w: pl.BlockSpec((tm, w), lambda b, i: (b * nt + i, c // w))
    full = lambda a: pl.BlockSpec(a.shape, lambda b, i: (0,) * a.ndim)
    row = pl.BlockSpec((tm, LANE), lambda b, i: (b * nt + i, 0))
    out_specs = [pl.BlockSpec((FOX_HEADS, tm, LANE), lambda b, i: (0, b * nt + i, 0)), row, row,
                 pl.BlockSpec((tm, FOX_HEADS), lambda b, i: (b * nt + i, 0))]
    out_shape = [jax.ShapeDtypeStruct((FOX_HEADS, m, LANE), F32), jax.ShapeDtypeStruct((m, LANE), F32),
                 jax.ShapeDtypeStruct((m, LANE), F32), jax.ShapeDtypeStruct((m, FOX_HEADS), F32)]
    scratch = []
    if with_cum:
        out_specs += [pl.BlockSpec((tm, FOX_HEADS), lambda b, i: (b * nt + i, 0)),
                      pl.BlockSpec((None, FOX_HEADS, tm), lambda b, i: (b, 0, i))]
        out_shape += [jax.ShapeDtypeStruct((m, FOX_HEADS), F32), jax.ShapeDtypeStruct((nb, FOX_HEADS, t), F32)]
        scratch = [pltpu.VMEM((1, LANE), F32)]
    return pl.pallas_call(
        functools.partial(_fox_pre_kernel, with_cum=with_cum),
        grid=(nb, nt),
        in_specs=[col(C_FQ, FOX_HEADS * FOX_HD), col(C_FK, LANE), col(C_FV, LANE), col(C_FL, LANE),
                  full(gq), full(gk), full(fb)],
        out_specs=out_specs,
        out_shape=out_shape,
        scratch_shapes=scratch,
        compiler_params=_cp(("parallel", "arbitrary")),
        name="fox_pre",
    )(u, u, u, u, gq, gk, fb)


def _causal_attn_kernel(*refs, scale, tq, tk, nheads, dv, has_bias, v_from_k, concat_heads):
    it = iter(refs)
    q_ref, k_ref = next(it), next(it)
    v_ref = k_ref if v_from_k else next(it)
    cc_ref, cr_ref = (next(it), next(it)) if has_bias else (None, None)
    o_ref, qs_ref, m_ref, l_ref, acc_ref = next(it), next(it), next(it), next(it), next(it)
    qi, ki = pl.program_id(1), pl.program_id(2)

    @pl.when(ki == 0)
    def _():
        m_ref[...] = jnp.full(m_ref.shape, -jnp.inf, F32)
        l_ref[...] = jnp.zeros_like(l_ref)
        acc_ref[...] = jnp.zeros_like(acc_ref)
        qs_ref[...] = q_ref[...].astype(BF16)

    @pl.when(ki * tk <= qi * tq + tq - 1)
    def _():
        k = k_ref[...].astype(BF16)
        v = v_ref[:, 0:dv].astype(BF16)
        row = qi * tq + lax.broadcasted_iota(jnp.int32, (tq, tk), 0)
        colk = ki * tk + lax.broadcasted_iota(jnp.int32, (tq, tk), 1)
        visible = colk <= row
        for h in range(nheads):
            s = _dot_nt(qs_ref[h], k) * scale
            if has_bias:
                s = s + (cc_ref[:, h:h + 1] - cr_ref[h:h + 1, :])
            s = jnp.where(visible, s, -jnp.inf)
            m_prev = m_ref[h]
            m_new = jnp.maximum(m_prev, jnp.max(s, axis=-1, keepdims=True))
            alpha = jnp.exp(m_prev - m_new)
            p = jnp.exp(s - m_new)
            l_ref[h] = alpha * l_ref[h] + jnp.sum(p, axis=-1, keepdims=True)
            acc_ref[h] = alpha * acc_ref[h] + jnp.dot(p.astype(BF16), v, preferred_element_type=F32)
            m_ref[h] = m_new

    @pl.when(ki == pl.num_programs(2) - 1)
    def _():
        for h in range(nheads):
            o = acc_ref[h] / l_ref[h]
            if concat_heads:
                o_ref[:, h * dv:(h + 1) * dv] = o
            else:
                o_ref[h] = o


def causal_attn(q, k, v, cum_col, cum_row, nb, scale, dv, tq, tk, concat_heads):
    nheads, m, dq = q.shape
    t = m // nb
    nq, nk = t // tq, t // tk
    kclamp = lambda qi, ki: jnp.minimum(ki, (qi * tq + tq - 1) // tk)
    in_specs = [pl.BlockSpec((nheads, tq, dq), lambda b, qi, ki: (0, b * nq + qi, 0)),
                pl.BlockSpec((tk, k.shape[1]), lambda b, qi, ki: (b * nk + kclamp(qi, ki), 0))]
    args = [q, k]
    if v is not None:
        in_specs.append(pl.BlockSpec((tk, v.shape[1]), lambda b, qi, ki: (b * nk + kclamp(qi, ki), 0)))
        args.append(v)
    if cum_col is not None:
        in_specs += [pl.BlockSpec((tq, nheads), lambda b, qi, ki: (b * nq + qi, 0)),
                     pl.BlockSpec((None, nheads, tk), lambda b, qi, ki: (b, 0, kclamp(qi, ki)))]
        args += [cum_col, cum_row]
    if concat_heads:
        out_spec = pl.BlockSpec((tq, nheads * dv), lambda b, qi, ki: (b * nq + qi, 0))
        out_shape = jax.ShapeDtypeStruct((m, nheads * dv), F32)
    else:
        out_spec = pl.BlockSpec((nheads, tq, dv), lambda b, qi, ki: (0, b * nq + qi, 0))
        out_shape = jax.ShapeDtypeStruct((nheads, m, dv), F32)
    return pl.pallas_call(
        functools.partial(_causal_attn_kernel, scale=scale, tq=tq, tk=tk, nheads=nheads, dv=dv,
                          has_bias=cum_col is not None, v_from_k=v is None, concat_heads=concat_heads),
        grid=(nb, nq, nk),
        in_specs=in_specs,
        out_specs=out_spec,
        out_shape=out_shape,
        scratch_shapes=[pltpu.VMEM((nheads, tq, dq), BF16), pltpu.VMEM((nheads, tq, 1), F32),
                        pltpu.VMEM((nheads, tq, 1), F32), pltpu.VMEM((nheads, tq, dv), F32)],
        compiler_params=_cp(("parallel", "parallel", "arbitrary")),
        name="causal_attn",
    )(*args)


def _causal_conv(x_ref, c0_ref, w_ref, xbuf_ref, cnew_ref, tt):
    lo = SUBLANE - (CONV_W - 1)

    @pl.when(pl.program_id(1) == 0)
    def _():
        xbuf_ref[:, lo:SUBLANE, :] = c0_ref[...]
    xbuf_ref[:, SUBLANE:SUBLANE + tt, :] = x_ref[...]
    y = xbuf_ref[:, lo:lo + tt, :] * w_ref[0:1, :]
    for j in range(1, CONV_W):
        y = y + xbuf_ref[:, lo + j:lo + j + tt, :] * w_ref[j:j + 1, :]
    tail = xbuf_ref[:, lo + tt:SUBLANE + tt, :]
    cnew_ref[...] = tail
    xbuf_ref[:, lo:SUBLANE, :] = tail
    return y


def _lru_kernel(x_ref, gate_ref, c0_ref, h0_ref, cw_ref, cb_ref, wa_ref, ba_ref, wx_ref, bx_ref, lam_ref,
                y_ref, hl_ref, cnew_ref, xbuf_ref, hcar_ref, abuf_ref, bbuf_ref, *, tt):
    bb = x_ref.shape[0]

    @pl.when(pl.program_id(1) == 0)
    def _():
        hcar_ref[...] = h0_ref[...]
        abuf_ref[:, 0:tt, :] = jnp.ones((bb, tt, LRU_WIDTH), F32)
        bbuf_ref[:, 0:tt, :] = jnp.zeros((bb, tt, LRU_WIDTH), F32)

    xc = _causal_conv(x_ref, c0_ref, cw_ref, xbuf_ref, cnew_ref, tt) + cb_ref[...]
    xc2 = xc.reshape(bb * tt, LRU_WIDTH)
    bw = LRU_WIDTH // LRU_BLOCKS
    ra = jnp.concatenate([_dot(xc2[:, n * bw:(n + 1) * bw], wa_ref[n]) for n in range(LRU_BLOCKS)], axis=-1)
    rx = jnp.concatenate([_dot(xc2[:, n * bw:(n + 1) * bw], wx_ref[n]) for n in range(LRU_BLOCKS)], axis=-1)
    r = _sigmoid(ra + ba_ref[...])
    i = _sigmoid(rx + bx_ref[...])
    log_a = -LRU_C * r * _softplus(-lam_ref[...])
    a = jnp.exp(log_a).reshape(bb, tt, LRU_WIDTH)
    nem1 = -jnp.tanh(log_a) * (jnp.exp(2.0 * log_a) + 1.0)
    b = (jnp.sqrt(nem1) * (i * xc2)).reshape(bb, tt, LRU_WIDTH)
    s = 1
    while s < tt:
        abuf_ref[:, tt:2 * tt, :] = a
        bbuf_ref[:, tt:2 * tt, :] = b
        b = b + a * bbuf_ref[:, tt - s:2 * tt - s, :]
        a = a * abuf_ref[:, tt - s:2 * tt - s, :]
        s *= 2
    h = a * hcar_ref[...] + b
    hl = h[:, tt - 1:tt, :]
    hcar_ref[...] = hl
    hl_ref[...] = hl
    g = gate_ref[...]
    gelu = 0.5 * g * (1.0 + jnp.tanh(math.sqrt(2.0 / math.pi) * (g + 0.044715 * (g * g * g))))
    y_ref[...] = h * gelu


def lru_mixer(u3, h0, conv0, cw, cb, wa, ba, wx, bx, lam, bb, tt):
    nb, t, _ = u3.shape
    w = LRU_WIDTH
    full = lambda a: pl.BlockSpec(a.shape, lambda b, i: (0,) * a.ndim)
    blk = lambda c: pl.BlockSpec((bb, tt, w), lambda b, i: (b, i, c // w))
    st = lambda r: pl.BlockSpec((bb, r, w), lambda b, i: (b, 0, 0))
    return pl.pallas_call(
        functools.partial(_lru_kernel, tt=tt),
        grid=(nb // bb, t // tt),
        in_specs=[blk(C_LX), blk(C_LG), st(CONV_W - 1), st(1), full(cw), full(cb), full(wa), full(ba), full(wx),
                  full(bx), full(lam)],
        out_specs=[pl.BlockSpec((bb, tt, w), lambda b, i: (b, i, 0)), st(1), st(CONV_W - 1)],
        out_shape=[jax.ShapeDtypeStruct((nb, t, w), F32), jax.ShapeDtypeStruct((nb, 1, w), F32),
                   jax.ShapeDtypeStruct((nb, CONV_W - 1, w), F32)],
        scratch_shapes=[pltpu.VMEM((bb, SUBLANE + tt, w), F32), pltpu.VMEM((bb, 1, w), F32),
                        pltpu.VMEM((bb, 2 * tt, w), F32), pltpu.VMEM((bb, 2 * tt, w), F32)],
        compiler_params=_cp(("parallel", "arbitrary")),
        name="lru_mixer",
    )(u3, u3, conv0, h0, cw, cb, wa, ba, wx, bx, lam)


def _gdn_pre_kernel(xq_ref, xk_ref, xv_ref, gin_ref, cq_ref, ck_ref, cv_ref, wq_ref, wk_ref, wv_ref, alog_ref, dtb_ref,
                    q_ref, k_ref, v_ref, bg_ref, nq_ref, nk_ref, nv_ref, bq_ref, bk_ref, bv_ref, *, tt):
    groups = ((xq_ref, cq_ref, wq_ref, bq_ref, nq_ref, q_ref, GDN_DK ** -0.5),
              (xk_ref, ck_ref, wk_ref, bk_ref, nk_ref, k_ref, 1.0),
              (xv_ref, cv_ref, wv_ref, bv_ref, nv_ref, v_ref, None))
    for x_ref, c0_ref, w_ref, xbuf_ref, cnew_ref, o_ref, mult in groups:
        y = _causal_conv(x_ref, c0_ref, w_ref, xbuf_ref, cnew_ref, tt)
        y = y * _sigmoid(y)
        if mult is None:
            o_ref[...] = y
        else:
            for h in range(GDN_HEADS):
                yh = y[:, :, h * LANE:(h + 1) * LANE]
                yn = yh * lax.rsqrt(jnp.sum(yh * yh, axis=-1, keepdims=True) + NORM_EPS)
                o_ref[:, :, h * LANE:(h + 1) * LANE] = yn * mult if mult != 1.0 else yn
    x = gin_ref[...]
    lane = lax.broadcasted_iota(jnp.int32, x.shape, 2)
    g = -jnp.exp(alog_ref[...]) * _softplus(x + dtb_ref[...])
    bg_ref[...] = jnp.where(lane < GDN_HEADS, _sigmoid(x), jnp.where(lane < 2 * GDN_HEADS, g, 0.0))


def gdn_pre(u3, conv0, cw, alog, dtb, bb, tt):
    nb, t, _ = u3.shape
    w = GDN_HEADS * GDN_DK
    full = lambda a: pl.BlockSpec(a.shape, lambda b, i: (0,) * a.ndim)
    blk = lambda c: pl.BlockSpec((bb, tt, w), lambda b, i: (b, i, c // w))
    st = lambda k: pl.BlockSpec((bb, CONV_W - 1, w), lambda b, i: (b, 0, k))
    wsp = lambda k: pl.BlockSpec((CONV_W, w), lambda b, i: (0, k))
    oblk = pl.BlockSpec((bb, tt, w), lambda b, i: (b, i, 0))
    ost = pl.BlockSpec((bb, CONV_W - 1, w), lambda b, i: (b, 0, 0))
    return pl.pallas_call(
        functools.partial(_gdn_pre_kernel, tt=tt),
        grid=(nb // bb, t // tt),
        in_specs=[blk(C_GQ), blk(C_GQ + w), blk(C_GQ + 2 * w),
                  pl.BlockSpec((bb, tt, LANE), lambda b, i: (b, i, C_GB // LANE)),
                  st(0), st(1), st(2), wsp(0), wsp(1), wsp(2), full(alog), full(dtb)],
        out_specs=[oblk, oblk, oblk, pl.BlockSpec((bb, tt, LANE), lambda b, i: (b, i, 0)), ost, ost, ost],
        out_shape=[jax.ShapeDtypeStruct((nb, t, w), F32)] * 3 + [jax.ShapeDtypeStruct((nb, t, LANE), F32)]
        + [jax.ShapeDtypeStruct((nb, CONV_W - 1, w), F32)] * 3,
        scratch_shapes=[pltpu.VMEM((bb, SUBLANE + tt, w), F32)] * 3,
        compiler_params=_cp(("parallel", "arbitrary")),
        name="gdn_pre",
    )(u3, u3, u3, u3, conv0, conv0, conv0, cw, cw, cw, alog, dtb)


def _gdn_chunk_kernel(q_ref, k_ref, v_ref, bg_ref, z_ref, s0_ref, ng_ref, o_ref, s_ref, *, c, nchunk):
    bb = q_ref.shape[0]
    nblk = bb * GDN_HEADS
    r = nblk * c

    @pl.when(pl.program_id(1) == 0)
    def _():
        s_ref[...] = s0_ref[...]

    ri = lax.broadcasted_iota(jnp.int32, (r, r), 0)
    ci = lax.broadcasted_iota(jnp.int32, (r, r), 1)
    same = (ri // c) == (ci // c)
    causal = jnp.logical_and(same, ri >= ci)
    strict = jnp.logical_and(same, ri > ci)
    last = jnp.logical_and(same, lax.rem(ci, c) == c - 1)
    eye = (ri == ci).astype(F32)

    def stack(ref, rows, lanes=None):
        return jnp.concatenate([ref[b, rows, h * LANE:(h + 1) * LANE] if lanes is None else ref[b, rows, lanes + h:lanes + h + 1]
                                for b in range(bb) for h in range(GDN_HEADS)], axis=0)

    for g in range(nchunk):
        rows = slice(g * c, (g + 1) * c)
        q, k, v, z = stack(q_ref, rows), stack(k_ref, rows), stack(v_ref, rows), stack(z_ref, rows)
        beta, glog = stack(bg_ref, rows, 0), stack(bg_ref, rows, GDN_HEADS)
        gb = jnp.broadcast_to(glog, (r, LANE))
        g_hi = gb.astype(BF16)
        g_mid, g_lo = _split(gb - g_hi.astype(F32))
        cb = causal.astype(BF16)
        gc = (jnp.dot(cb, g_hi, preferred_element_type=F32) + (jnp.dot(cb, g_mid, preferred_element_type=F32)
              + jnp.dot(cb, g_lo, preferred_element_type=F32)))[:, 0:1]
        gc_row = jnp.sum(eye * gc, axis=0, keepdims=True)
        decay = jnp.where(causal, jnp.exp(jnp.where(causal, gc - gc_row, 0.0)), 0.0)
        g_last = jnp.sum(jnp.where(last, gc_row, 0.0), axis=1, keepdims=True)
        kb = k * beta
        lower = jnp.where(strict, _dot_nt(kb, k) * decay, 0.0)
        inv = eye - lower
        lsp = _split(lower)
        pw = _dot3(lsp, lsp)
        n = 2
        while n < c:
            psp = _split(pw)
            inv = inv + _dot3(_split(inv), psp)
            n *= 2
            if n < c:
                pw = _dot3(psp, psp)
        uw = _dot3(_split(inv), _split(jnp.concatenate([v * beta, kb * jnp.exp(gc)], axis=1)))
        uu, ww = uw[:, 0:LANE], uw[:, LANE:2 * LANE]
        qk = jnp.where(causal, _dot_nt(q, k) * decay, 0.0)
        qg = q * jnp.exp(gc)
        kd = k * jnp.exp(g_last - gc)
        eg = jnp.exp(g_last)
        ws, qs = [], []
        for blk in range(nblk):
            br = slice(blk * c, (blk + 1) * c)
            both = _dot(jnp.concatenate([ww[br], qg[br]], axis=0), s_ref[blk // GDN_HEADS, blk % GDN_HEADS])
            ws.append(both[0:c])
            qs.append(both[c:2 * c])
        v_new = uu - jnp.concatenate(ws, axis=0)
        o = jnp.concatenate(qs, axis=0) + _dot(qk, v_new)
        o = _rms(o, ng_ref[...]) * (z * _sigmoid(z))
        for blk in range(nblk):
            br = slice(blk * c, (blk + 1) * c)
            b, h = blk // GDN_HEADS, blk % GDN_HEADS
            s_ref[b, h] = s_ref[b, h] * eg[blk * c:blk * c + 1, :] + lax.dot_general(
                kd[br].astype(BF16), v_new[br].astype(BF16), (((0,), (0,)), ((), ())), preferred_element_type=F32)
            o_ref[b, rows, h * LANE:(h + 1) * LANE] = o[br]


def gdn_chunk(q, k, v, bg, u3, s0, ng, c, bb, nchunk):
    nb, t, w = q.shape
    tt = c * nchunk
    blk = pl.BlockSpec((bb, tt, w), lambda b, n: (b, n, 0))
    sblk = pl.BlockSpec((bb, GDN_HEADS, GDN_DK, GDN_DV), lambda b, n: (b, 0, 0, 0))
    return pl.pallas_call(
        functools.partial(_gdn_chunk_kernel, c=c, nchunk=nchunk),
        grid=(nb // bb, t // tt),
        in_specs=[blk, blk, blk, pl.BlockSpec((bb, tt, LANE), lambda b, n: (b, n, 0)),
                  pl.BlockSpec((bb, tt, w), lambda b, n: (b, n, C_GZ // w)), sblk,
                  pl.BlockSpec(ng.shape, lambda b, n: (0, 0))],
        out_specs=[blk, sblk],
        out_shape=[jax.ShapeDtypeStruct((nb, t, w), F32), jax.ShapeDtypeStruct(s0.shape, F32)],
        compiler_params=_cp(("parallel", "arbitrary")),
        name="gdn_chunk",
    )(q, k, v, bg, u3, s0, ng)


def _paged_attn_kernel(pt_ref, *refs, kind, layer, pp, nc, npages, tnew, scale):
    nheads = MLA_HEADS
    rows = nheads * tnew
    if kind == "mla":
        (q_ref, kn_ref, rn_ref, wuv_ref, ck_hbm, cr_hbm, o_ref,
         kbuf, rbuf, sem, m_ref, l_ref, acc_ref) = refs
        streams = ((ck_hbm, kbuf), (cr_hbm, rbuf))
    else:
        (q_ref, kn_ref, vn_ref, lfn_ref, ck_hbm, cv_hbm, clf_hbm, o_ref,
         kbuf, vbuf, sem, lfbuf, lfsem, cum_ref, tot_ref, cq_ref, m_ref, l_ref, acc_ref) = refs
        streams = ((ck_hbm, kbuf), (cv_hbm, vbuf))
    b, c = pl.program_id(0), pl.program_id(1)
    nb = pl.num_programs(0)
    step = b * nc + c
    slot = lax.rem(step, 2)

    def page_copies(bb, cc, sl):
        out = []
        for p in range(pp):
            page = pt_ref[bb * npages + cc * pp + p]
            for si, (hbm, buf) in enumerate(streams):
                out.append(pltpu.make_async_copy(hbm.at[layer, page], buf.at[sl, p], sem.at[sl, si]))
        return out

    def logf_copies(bb, sl):
        return [pltpu.make_async_copy(clf_hbm.at[layer, pt_ref[bb * npages + p]],
                                      lfbuf.at[sl, pl.ds(FOX_HEADS * p, FOX_HEADS)], lfsem.at[sl]) for p in range(npages)]

    @pl.when(step == 0)
    def _():
        for cp in page_copies(0, 0, 0):
            cp.start()
        if kind == "fox":
            for cp in logf_copies(0, 0):
                cp.start()

    @pl.when(step + 1 < nb * nc)
    def _():
        nxt = step + 1
        for cp in page_copies(nxt // nc, lax.rem(nxt, nc), 1 - slot):
            cp.start()

    if kind == "fox":
        @pl.when(jnp.logical_and(c == 0, b + 1 < nb))
        def _():
            for cp in logf_copies(b + 1, 1 - lax.rem(b, 2)):
                cp.start()

    q = q_ref[...].reshape(rows, q_ref.shape[-1])
    rowq = lax.rem(lax.broadcasted_iota(jnp.int32, (rows, LANE), 0), tnew)
    lanei = lax.broadcasted_iota(jnp.int32, (rows, LANE), 1)

    @pl.when(c == 0)
    def _():
        m_ref[...] = jnp.full(m_ref.shape, -jnp.inf, F32)
        l_ref[...] = jnp.zeros_like(l_ref)
        acc_ref[...] = jnp.zeros_like(acc_ref)
        if kind == "fox":
            bs = lax.rem(b, 2)
            for cp in logf_copies(b, bs):
                cp.wait()
            x = lfbuf[bs]
            ti = lax.broadcasted_iota(jnp.int32, (PAGE, PAGE), 0)
            tj = lax.broadcasted_iota(jnp.int32, (PAGE, PAGE), 1)
            within = _dot_hi(x, (ti <= tj).astype(F32))
            tot = jnp.broadcast_to(within[:, PAGE - 1:PAGE], within.shape)
            n4 = npages * FOX_HEADS
            rowi = lax.broadcasted_iota(jnp.int32, (n4, PAGE), 0)
            incl = tot
            sh = FOX_HEADS
            while sh < n4:
                incl = incl + jnp.where(rowi >= sh, pltpu.roll(incl, sh, 0), 0.0)
                sh *= 2
            offs = incl - tot
            cum_ref[...] = (within + offs).reshape(cum_ref.shape)
            total = offs[n4 - FOX_HEADS:n4, 0:1] + tot[n4 - FOX_HEADS:n4, 0:1]
            lfn = lfn_ref[...]
            ncum = lfn
            sh = 1
            while sh < tnew:
                ncum = ncum + jnp.where(lax.broadcasted_iota(jnp.int32, ncum.shape, 0) >= sh, pltpu.roll(ncum, sh, 0), 0.0)
                sh *= 2
            for h in range(nheads):
                cn = total[h:h + 1, :] + ncum[:, h:h + 1]
                cq_ref[h * tnew:(h + 1) * tnew, :] = cn
                eye = (lax.broadcasted_iota(jnp.int32, (tnew, LANE), 0) == lax.broadcasted_iota(jnp.int32, (tnew, LANE), 1))
                tot_ref[h:h + 1, :] = jnp.sum(jnp.where(eye, cn, 0.0), axis=0, keepdims=True)

    for cp in page_copies(b, c, slot):
        cp.wait()

    def online_update(s, v):
        m_prev = m_ref[...]
        m_new = jnp.maximum(m_prev, jnp.max(s, axis=-1, keepdims=True))
        alpha = jnp.exp(m_prev - m_new)
        p = jnp.exp(s - m_new)
        l_ref[...] = alpha * l_ref[...] + jnp.sum(p, axis=-1, keepdims=True)
        acc_ref[...] = alpha * acc_ref[...] + jnp.dot(p.astype(BF16), v, preferred_element_type=F32)
        m_ref[...] = m_new

    keys = kbuf[slot].reshape(pp * PAGE, kbuf.shape[-1]).astype(BF16)
    if kind == "mla":
        qpe = q[:, LANE:LANE + MLA_ROPE].astype(BF16)
        s_rope = jnp.concatenate([jnp.dot(qpe, rbuf[slot, p].astype(BF16), preferred_element_type=F32) for p in range(pp)],
                                 axis=1)
        s = (_dot_nt(q[:, 0:LANE], keys) + s_rope) * scale
        online_update(s, keys)
    else:
        vals = vbuf[slot].reshape(pp * PAGE, FOX_HD).astype(BF16)
        s = _dot_nt(q, keys) * scale
        ck = jnp.concatenate(
            [jnp.concatenate([jnp.broadcast_to(cum_ref[c, FOX_HEADS * p + h:FOX_HEADS * p + h + 1, :], (tnew, PAGE))
                              for h in range(nheads)], axis=0) for p in range(pp)], axis=1)
        online_update(s + (cq_ref[...] - ck), vals)

    @pl.when(c == nc - 1)
    def _():
        zpad = jnp.zeros((LANE - tnew, LANE), F32)
        kn = jnp.concatenate([kn_ref[...], zpad], axis=0)
        if kind == "mla":
            rn = jnp.concatenate([rn_ref[...], jnp.zeros((LANE - tnew, MLA_ROPE), F32)], axis=0)
            s = (_dot_nt(q[:, 0:LANE], kn) + _dot_nt(q[:, LANE:LANE + MLA_ROPE], rn)) * scale
            vn = kn
        else:
            vn = jnp.concatenate([vn_ref[...], zpad], axis=0)
            s = _dot_nt(q, kn) * scale
            ckn = jnp.concatenate([jnp.broadcast_to(tot_ref[h:h + 1, :], (tnew, LANE)) for h in range(nheads)], axis=0)
            s = s + (cq_ref[...] - ckn)
        s = jnp.where(lanei <= rowq, s, -jnp.inf)
        online_update(s, vn.astype(BF16))
        o = acc_ref[...] / l_ref[...]
        for h in range(nheads):
            oh = o[h * tnew:(h + 1) * tnew, :]
            o_ref[:, h * LANE:(h + 1) * LANE] = _dot(oh, wuv_ref[h]) if kind == "mla" else oh


def paged_attn(kind, layer, page_table, q, new_parts, caches, wuv, scale, pp):
    nheads, m, dq = q.shape
    nbatch, npages = page_table.shape
    tnew = m // nbatch
    nc = npages // pp
    rows = nheads * tnew
    qspec = pl.BlockSpec((nheads, tnew, dq), lambda b, c, pt: (0, b, 0))
    newspec = lambda a: pl.BlockSpec((tnew, a.shape[1]), lambda b, c, pt: (b, 0))
    anyspec = pl.BlockSpec(memory_space=pl.ANY)
    in_specs = [qspec] + [newspec(a) for a in new_parts]
    args = [q] + list(new_parts)
    if kind == "mla":
        in_specs.append(pl.BlockSpec(wuv.shape, lambda b, c, pt: (0, 0, 0)))
        args.append(wuv)
        scratch = [pltpu.VMEM((2, pp, PAGE, MLA_KV_LORA), F32), pltpu.VMEM((2, pp, MLA_ROPE, PAGE), F32),
                   pltpu.SemaphoreType.DMA((2, 2))]
    else:
        scratch = [pltpu.VMEM((2, pp, PAGE, FOX_HD), F32), pltpu.VMEM((2, pp, PAGE, FOX_HD), F32),
                   pltpu.SemaphoreType.DMA((2, 2)),
                   pltpu.VMEM((2, npages * FOX_HEADS, PAGE), F32), pltpu.SemaphoreType.DMA((2,)),
                   pltpu.VMEM((nc, pp * FOX_HEADS, PAGE), F32), pltpu.VMEM((SUBLANE, LANE), F32),
                   pltpu.VMEM((rows, 1), F32)]
    in_specs += [anyspec] * len(caches)
    args += list(caches)
    scratch += [pltpu.VMEM((rows, 1), F32), pltpu.VMEM((rows, 1), F32), pltpu.VMEM((rows, LANE), F32)]
    return pl.pallas_call(
        functools.partial(_paged_attn_kernel, kind=kind, layer=layer, pp=pp, nc=nc, npages=npages, tnew=tnew, scale=scale),
        grid_spec=pltpu.PrefetchScalarGridSpec(
            num_scalar_prefetch=1, grid=(nbatch, nc), in_specs=in_specs,
            out_specs=pl.BlockSpec((tnew, nheads * LANE), lambda b, c, pt: (b, 0)),
            scratch_shapes=scratch),
        out_shape=jax.ShapeDtypeStruct((m, nheads * LANE), F32),
        compiler_params=_cp(("arbitrary", "arbitrary")),
        name="paged_attn_" + kind,
    )(page_table.reshape(-1), *args)


def _pad_lanes(a, width, offset=0):
    a = a.reshape(1, -1)
    return jnp.pad(a, ((0, 0), (offset, width - offset - a.shape[1])))


def _layer_params(l, w_in, mla_q_a_g, mla_w_uq, mla_q_norm_g, mla_kv_norm_g, mla_kpe_norm_g, mla_w_uk, mla_w_uv,
                  fox_q_norm_g, fox_k_norm_g, fox_f_bias, lru_conv_w, lru_conv_b, lru_w_a, lru_b_a, lru_w_x, lru_b_x,
                  lru_lambda, gdn_conv_w, gdn_a_log, gdn_dt_bias, gdn_norm_g, w_out, w_gate, w_up, w_down):
    w = jnp.swapaxes(w_in[l], 0, 1)
    d = w.shape[1]
    z = lambda n: jnp.zeros((n, d), w.dtype)
    n_qkv = MLA_Q_LORA + MLA_KV_LORA
    n_mla = n_qkv + MLA_ROPE
    n_fq = FOX_HEADS * FOX_HD
    n_fox = n_fq + 2 * FOX_HD + FOX_HEADS
    n_main = w.shape[0] - 2 * GDN_HEADS
    w_in_p = jnp.concatenate([w[:n_qkv], w[n_mla:n_mla + n_fq], w[n_qkv:n_mla], z(LANE - MLA_ROPE),
                              w[n_mla + n_fq:n_mla + n_fox], z(LANE - FOX_HEADS),
                              w[n_mla + n_fox:n_main], w[n_main:], z(D_INP - C_GB - 2 * GDN_HEADS)], axis=0).astype(BF16)
    uq = mla_w_uq[l]
    wuq = jnp.concatenate([uq[:, :, :MLA_NOPE].reshape(MLA_Q_LORA, -1),
                           jnp.pad(uq[:, :, MLA_NOPE:], ((0, 0), (0, 0), (0, LANE - MLA_ROPE))).reshape(MLA_Q_LORA, -1)],
                          axis=1).astype(BF16)
    return dict(
        w_in=w_in_p,
        gqa=mla_q_a_g[l].reshape(1, -1), wuq=wuq,
        gqn=mla_q_norm_g[l, :MLA_NOPE].reshape(1, -1), gqr=_pad_lanes(mla_q_norm_g[l, MLA_NOPE:], LANE),
        gkv=mla_kv_norm_g[l].reshape(1, -1), gkpe=_pad_lanes(mla_kpe_norm_g[l], LANE),
        wuk=jnp.transpose(mla_w_uk[l], (1, 2, 0)).astype(BF16), wuv=jnp.transpose(mla_w_uv[l], (1, 0, 2)).astype(BF16),
        fgq=fox_q_norm_g[l].reshape(1, -1), fgk=fox_k_norm_g[l].reshape(1, -1), fb=_pad_lanes(fox_f_bias[l], LANE),
        lcw=lru_conv_w[l], lcb=lru_conv_b[l].reshape(1, -1), lwa=lru_w_a[l].astype(BF16), lba=lru_b_a[l].reshape(1, -1),
        lwx=lru_w_x[l].astype(BF16), lbx=lru_b_x[l].reshape(1, -1), lam=lru_lambda[l].reshape(1, -1),
        gcw=gdn_conv_w[l], alog=_pad_lanes(gdn_a_log[l], LANE, GDN_HEADS), dtb=_pad_lanes(gdn_dt_bias[l], LANE, GDN_HEADS),
        gng=gdn_norm_g[l].reshape(1, -1),
        w_out=w_out, w_gate=w_gate, w_up=w_up, w_down=w_down,
    )


def _tiles(t):
    if t >= 512:
        return dict(tm=512, tm_mm=1024, bb=1, tt=512, tq=256, tk=512, gdn_bb=1, gdn_chunks=4)
    return dict(tm=512, tm_mm=512, bb=512 // t, tt=t, tq=None, tk=None, gdn_bb=8, gdn_chunks=1)


def _trunk_layer(l, p, x, mods, nb, t, g_mix, g_ffn, tabs, past):
    sh1, sc1, g1, sh2, sc2, g2 = mods
    m = nb * t
    ti = _tiles(t)
    tm = ti["tm"]
    u = norm_mod_matmul(x, g_mix, sc1, sh1, p["w_in"], t, ti["tm_mm"], 1024)
    u3 = u.reshape(nb, t, D_INP)
    q, lat, rot, kfull = mla_pre(u, tabs, tabs[0].shape[0] // tm, p["gqa"], p["wuq"], p["gqn"], p["gqr"], p["gkv"],
                                 p["gkpe"], p["wuk"], tm)
    mla_scale = (MLA_NOPE + MLA_ROPE) ** -0.5
    fox_scale = FOX_HD ** -0.5
    if past is None:
        o_lat = causal_attn(q, kfull, None, None, None, nb, mla_scale, MLA_KV_LORA, ti["tq"], ti["tk"], False)
        o_mla = head_proj(o_lat, p["wuv"], tm)
        fq, fk, fv, lf, ccol, crow = fox_pre(u, p["fgq"], p["fgk"], p["fb"], nb, tm, True)
        o_fox = causal_attn(fq, fk, fv, ccol, crow, nb, fox_scale, FOX_HD, ti["tq"], ti["tk"], True)
        h0 = jnp.zeros((nb, 1, LRU_WIDTH), F32)
        lconv0 = jnp.zeros((nb, CONV_W - 1, LRU_WIDTH), F32)
        s0 = jnp.zeros((nb, GDN_HEADS, GDN_DK, GDN_DV), F32)
        gconv0 = jnp.zeros((nb, CONV_W - 1, 3 * GDN_HEADS * GDN_DK), F32)
    else:
        page_table, c_lat, c_rope, c_k, c_v, c_lf, h0, lconv0, s0, gconv0 = past
        pp = min(PAGES_PER_STEP, page_table.shape[1])
        o_mla = paged_attn("mla", l, page_table, q, [lat, rot], [c_lat, c_rope], p["wuv"], mla_scale, pp)
        fq, fk, fv, lf = fox_pre(u, p["fgq"], p["fgk"], p["fb"], 1, tm, False)
        lf_pad = jnp.pad(lf, ((0, 0), (0, LANE - FOX_HEADS)))
        o_fox = paged_attn("fox", l, page_table, fq, [fk, fv, lf_pad], [c_k, c_v, c_lf], None, fox_scale, pp)
        h0 = h0[:, None, :]
    y_lru, h_new, lconv_new = lru_mixer(u3, h0, lconv0, p["lcw"], p["lcb"], p["lwa"], p["lba"], p["lwx"], p["lbx"],
                                        p["lam"], ti["bb"], ti["tt"])
    gq, gk, gv, bg, nq, nk, nv = gdn_pre(u3, gconv0, p["gcw"], p["alog"], p["dtb"], ti["bb"], ti["tt"])
    o_gdn, s_new = gdn_chunk(gq, gk, gv, bg, u3, s0, p["gng"], math.gcd(t, GDN_CHUNK), ti["gdn_bb"], ti["gdn_chunks"])
    w = MLA_HEADS * LANE
    x = matmul_residual([o_mla, o_fox, y_lru.reshape(m, w), o_gdn.reshape(m, w)], p["w_out"], l, x, g1, t, ti["tm_mm"], 512)
    a = ffn_up(x, g_ffn, sc2, sh2, p["w_gate"], p["w_up"], l, t, ti["tm_mm"], 512)
    x = matmul_residual([a], p["w_down"], l, x, g2, t, 512, 512)
    states = (lat.reshape(nb, t, -1), rot.reshape(nb, t, -1), fk.reshape(nb, t, 1, FOX_HD), fv.reshape(nb, t, 1, FOX_HD),
              lf.reshape(nb, t, FOX_HEADS), h_new.reshape(nb, LRU_WIDTH), lconv_new, s_new,
              jnp.concatenate([nq, nk, nv], axis=-1))
    return x, states


def kernel(x_prompt, x_sample, cache_mla_lat, cache_mla_rope, cache_fox_k, cache_fox_v, cache_fox_logf, state_lru_h, state_lru_conv, state_gdn_S, state_gdn_conv, page_table, c_prompt, c_sample, w_ada, b_ada, norm_mix_g, w_in, mla_q_a_g, mla_w_uq, mla_q_norm_g, mla_kv_norm_g, mla_kpe_norm_g, mla_w_uk, mla_w_uv, fox_q_norm_g, fox_k_norm_g, fox_f_bias, lru_conv_w, lru_conv_b, lru_w_a, lru_b_a, lru_w_x, lru_b_x, lru_lambda, gdn_conv_w, gdn_a_log, gdn_dt_bias, gdn_norm_g, w_out, norm_ffn_g, w_gate, w_up, w_down):
    bp, tp, d = x_prompt.shape
    bs, ts, _ = x_sample.shape
    n_pool = cache_mla_lat.shape[1]
    n_past = page_table.shape[1] * PAGE
    c_k = cache_fox_k.reshape(DEPTH, n_pool, PAGE, FOX_HD)
    c_v = cache_fox_v.reshape(DEPTH, n_pool, PAGE, FOX_HD)
    c_rope = jnp.swapaxes(cache_mla_rope, 2, 3)
    c_lf = jnp.swapaxes(cache_fox_logf, 2, 3)
    tabs_p = rope_tables(jnp.arange(tp, dtype=jnp.int32))
    tm_s = _tiles(ts)["tm"]
    tabs_s = rope_tables(n_past + jnp.arange(tm_s, dtype=jnp.int32) % ts)
    c_all = jnp.concatenate([c_sample, c_prompt], axis=0)
    x_p = x_prompt.reshape(bp * tp, d)
    x_s = x_sample.reshape(bs * ts, d)
    states_p, states_s = [], []
    for l in range(DEPTH):
        p = _layer_params(l, w_in, mla_q_a_g, mla_w_uq, mla_q_norm_g, mla_kv_norm_g, mla_kpe_norm_g, mla_w_uk, mla_w_uv,
                          fox_q_norm_g, fox_k_norm_g, fox_f_bias, lru_conv_w, lru_conv_b, lru_w_a, lru_b_a, lru_w_x,
                          lru_b_x, lru_lambda, gdn_conv_w, gdn_a_log, gdn_dt_bias, gdn_norm_g, w_out, w_gate, w_up, w_down)
        mod = ada_mod(c_all, w_ada, b_ada, l)
        mods = jnp.split(mod, 6, axis=-1)
        mods_p = [mm[bs:, None, :] for mm in mods]
        mods_s = [jnp.repeat(mm[:bs], ts, axis=0) for mm in mods]
        x_p, st_p = _trunk_layer(l, p, x_p, mods_p, bp, tp, norm_mix_g[l], norm_ffn_g[l], tabs_p, None)
        past = (page_table, cache_mla_lat, c_rope, c_k, c_v, c_lf, state_lru_h[l], state_lru_conv[l],
                state_gdn_S[l], state_gdn_conv[l])
        x_s, st_s = _trunk_layer(l, p, x_s, mods_s, bs, ts, norm_mix_g[l], norm_ffn_g[l], tabs_s, past)
        states_p.append(st_p)
        states_s.append(st_s)
    sp = [jnp.stack(a, axis=0) for a in zip(*states_p)]
    ss = [jnp.stack(a, axis=0) for a in zip(*states_s)]
    out = [x_p.reshape(bp, tp, d), x_s.reshape(bs, ts, d)]
    for a, b in zip(sp, ss):
        out += [a, b]
    return tuple(out)
```
